```python
import math
import jax, jax.numpy as jnp
from jax import lax
import numpy as np

D_MODEL = 1024
BATCH = 16
SEQ = 2048
DEPTH = 2

SSD_HEADS = 16
SSD_HEAD_DIM = 64
SSD_INNER = SSD_HEADS * SSD_HEAD_DIM
SSD_GROUPS = 2
SSD_STATE = 128
SSD_CONV = 4
SSD_CHUNK = 128
SSD_CONV_CH = SSD_INNER + 2 * SSD_GROUPS * SSD_STATE
SB_HEADS = 16
SB_HEAD_DIM = 64
SB_INNER = SB_HEADS * SB_HEAD_DIM
SB_BLOCK = 128
HY_SPLITS = [SSD_INNER, SSD_INNER + SSD_CONV_CH, SSD_INNER + SSD_CONV_CH + SSD_HEADS,
             SSD_INNER + SSD_CONV_CH + SSD_HEADS + SB_INNER,
             SSD_INNER + SSD_CONV_CH + SSD_HEADS + 2 * SB_INNER]
HY_IN_COLS = SSD_INNER + SSD_CONV_CH + SSD_HEADS + 3 * SB_INNER
HY_MIX = SSD_INNER + SB_INNER
ML_HEADS = 8
ML_QK_DIM = 64
ML_V_DIM = 128
ML_QK = ML_HEADS * ML_QK_DIM
ML_V = ML_HEADS * ML_V_DIM
ML_CHUNK = 128
ML_GATE_CAP = 15.0
ML_SPLITS = [ML_QK, 2 * ML_QK, 2 * ML_QK + ML_V, 2 * ML_QK + 2 * ML_V,
             2 * ML_QK + 2 * ML_V + ML_HEADS]
ML_IN_COLS = 2 * ML_QK + 2 * ML_V + 2 * ML_HEADS
D_FF = 2816
FFN_CONV = 3
RMS_EPS = 1e-6
N_EVEN = (DEPTH + 1) // 2
N_ODD = DEPTH // 2

kernel_name = "hybrid_ssd_stickbreak_mlstm_convffn"


def rmsnorm(x, w):
    xf = x.astype(jnp.float32)
    y = xf * lax.rsqrt(jnp.mean(xf * xf, axis=-1, keepdims=True) + RMS_EPS)
    return (y * w.astype(jnp.float32)).astype(x.dtype)


def causal_dwconv(x, w, b):
    k_width, ch = w.shape
    y = lax.conv_general_dilated(
        x, w[:, None, :].astype(x.dtype), window_strides=(1,),
        padding=[(k_width - 1, 0)], dimension_numbers=("NWC", "WIO", "NWC"),
        feature_group_count=ch)
    return y + b.astype(x.dtype)


def soft_cap(x, cap):
    return cap * jnp.tanh(x / cap)


def ssd_scan(xh, dt, a, bm, cm):
    bsz, seq, n_heads, p = xh.shape
    g, n = bm.shape[-2:]
    r = n_heads // g
    nc, l = seq // SSD_CHUNK, SSD_CHUNK
    x = (xh * dt[..., None]).reshape(bsz, nc, l, g, r, p)
    da = (dt * a).reshape(bsz, nc, l, g, r)
    bm = bm.reshape(bsz, nc, l, g, n)
    cm = cm.reshape(bsz, nc, l, g, n)
    a_cs = jnp.cumsum(da, axis=2)
    tri = jnp.tril(jnp.ones((l, l), dtype=bool))
    seg = a_cs[:, :, :, None] - a_cs[:, :, None, :]
    decay = jnp.exp(jnp.where(tri[:, :, None, None], seg, -jnp.inf))
    cb = jnp.einsum("bctgn,bcsgn->bctsg", cm, bm)
    y_diag = jnp.einsum("bctsg,bctsgr,bcsgrp->bctgrp", cb, decay, x)
    decay_end = jnp.exp(a_cs[:, :, -1:] - a_cs)
    chunk_states = jnp.einsum("bcsgn,bcsgr,bcsgrp->bcgrpn", bm, decay_end, x)
    chunk_decay = jnp.exp(a_cs[:, :, -1])

    def step(state, inp):
        dec, new = inp
        return dec[..., None, None] * state + new, state

    init = jnp.zeros((bsz, g, r, p, n), x.dtype)
    _, prev = lax.scan(step, init, (jnp.moveaxis(chunk_decay, 1, 0),
                                    jnp.moveaxis(chunk_states, 1, 0)))
    prev = jnp.moveaxis(prev, 0, 1)
    y_off = jnp.einsum("bctgn,bcgrpn,bctgr->bctgrp", cm, prev, jnp.exp(a_cs))
    return (y_diag + y_off).reshape(bsz, seq, n_heads, p)


def stick_breaking(q, k, v):
    bsz, n_heads, seq, dh = q.shape
    scale = dh ** -0.5
    outs = []
    for blk in range(seq // SB_BLOCK):
        t0 = blk * SB_BLOCK
        t1 = t0 + SB_BLOCK
        z = jnp.einsum("bhtd,bhsd->bhts", q[:, :, t0:t1], k[:, :, :t1]).astype(jnp.float32) * scale
        t_pos = t0 + jnp.arange(SB_BLOCK)
        s_pos = jnp.arange(t1)
        causal = s_pos[None, :] < t_pos[:, None]
        log_rem = jnp.where(causal, jax.nn.log_sigmoid(-z), 0.0)
        rem_after = lax.cumsum(log_rem, axis=3, reverse=True) - log_rem
        w = jnp.where(causal, jnp.exp(jax.nn.log_sigmoid(z) + rem_after), 0.0)
        outs.append(jnp.einsum("bhts,bhsd->bhtd", w.astype(v.dtype), v[:, :, :t1]))
    return jnp.concatenate(outs, axis=2)


def mlstm_chunkwise(q, k, v, logi, logf):
    bsz, seq, n_heads, kd = q.shape
    vd = v.shape[-1]
    nc, l = seq // ML_CHUNK, ML_CHUNK
    q = q.reshape(bsz, nc, l, n_heads, kd)
    k = k.reshape(bsz, nc, l, n_heads, kd)
    v = v.reshape(bsz, nc, l, n_heads, vd)
    logi = logi.reshape(bsz, nc, l, n_heads)
    logf = logf.reshape(bsz, nc, l, n_heads)
    b_cs = jnp.cumsum(logf, axis=2)
    g = b_cs[:, :, -1]
    w_end = g[:, :, None] - b_cs + logi
    m_loc = jnp.max(w_end, axis=2)
    e_end = jnp.exp(w_end - m_loc[:, :, None])
    loc_c = jnp.einsum("bcsh,bcshk,bcshv->bchkv", e_end, k, v)
    loc_n = jnp.einsum("bcsh,bcshk->bchk", e_end, k)

    def step(carry, inp):
        c_st, n_st, m_st = carry
        g_c, m_loc_c, lc, ln = inp
        m_new = jnp.maximum(g_c + m_st, m_loc_c)
        a_old = jnp.exp(g_c + m_st - m_new)
        a_loc = jnp.exp(m_loc_c - m_new)
        c_new = a_old[..., None, None] * c_st + a_loc[..., None, None] * lc
        n_new = a_old[..., None] * n_st + a_loc[..., None] * ln
        return (c_new, n_new, m_new), (c_st, n_st, m_st)

    init = (jnp.zeros((bsz, n_heads, kd, vd), q.dtype),
            jnp.zeros((bsz, n_heads, kd), q.dtype),
            jnp.zeros((bsz, n_heads), q.dtype))
    mv = lambda t: jnp.moveaxis(t, 1, 0)
    _, (c_prev, n_prev, m_prev) = lax.scan(step, init, (mv(g), mv(m_loc), mv(loc_c), mv(loc_n)))
    c_prev, n_prev, m_prev = mv(c_prev), mv(n_prev), mv(m_prev)
    bt = jnp.moveaxis(b_cs, 3, 2)
    it = jnp.moveaxis(logi, 3, 2)
    tri = jnp.tril(jnp.ones((l, l), dtype=bool))
    dmat = jnp.where(tri, bt[..., :, None] - bt[..., None, :] + it[..., None, :], -jnp.inf)
    inter = bt + m_prev[..., None]
    m_t = jnp.maximum(inter, jnp.max(dmat, axis=-1))
    sw = jnp.exp(dmat - m_t[..., None]) * jnp.einsum("bcthk,bcshk->bchts", q, k)
    inter_w = jnp.exp(inter - m_t)
    num = (jnp.einsum("bchts,bcshv->bcthv", sw, v)
           + jnp.einsum("bcthk,bchkv->bcthv", q, c_prev) * jnp.moveaxis(inter_w, 2, 3)[..., None])
    den = jnp.sum(sw, axis=-1) + jnp.einsum("bcthk,bchk->bcht", q, n_prev) * inter_w
    denom = jnp.maximum(jnp.abs(den), jnp.exp(-m_t))
    h = num / jnp.moveaxis(denom, 2, 3)[..., None]
    return h.reshape(bsz, seq, n_heads, vd)


def ssd_stickbreak_mixer(h, in_w, conv_w, conv_b, dt_bias, a_log, d_skip, ssd_norm_w,
                         q_norm_w, k_norm_w, out_w):
    bsz, seq, _ = h.shape
    f32 = jnp.float32
    z, xbc, dt_raw, q, k, v = jnp.split(h @ in_w, HY_SPLITS, axis=-1)
    xbc = jax.nn.silu(causal_dwconv(xbc, conv_w, conv_b))
    xs, bm, cm = jnp.split(xbc, [SSD_INNER, SSD_INNER + SSD_GROUPS * SSD_STATE], axis=-1)
    dt = jax.nn.softplus(dt_raw.astype(f32) + dt_bias.astype(f32))
    a = -jnp.exp(a_log.astype(f32))
    xh = xs.reshape(bsz, seq, SSD_HEADS, SSD_HEAD_DIM).astype(f32)
    y = ssd_scan(xh, dt, a,
                 bm.reshape(bsz, seq, SSD_GROUPS, SSD_STATE).astype(f32),
                 cm.reshape(bsz, seq, SSD_GROUPS, SSD_STATE).astype(f32))
    y = (y + xh * d_skip.astype(f32)[:, None]).reshape(bsz, seq, SSD_INNER).astype(h.dtype)
    y_ssd = rmsnorm(y * jax.nn.silu(z), ssd_norm_w)
    heads = lambda t: t.reshape(bsz, seq, SB_HEADS, SB_HEAD_DIM)
    qh = rmsnorm(heads(q), q_norm_w).transpose(0, 2, 1, 3)
    kh = rmsnorm(heads(k), k_norm_w).transpose(0, 2, 1, 3)
    vh = heads(v).transpose(0, 2, 1, 3)
    y_sb = stick_breaking(qh, kh, vh).transpose(0, 2, 1, 3).reshape(bsz, seq, SB_INNER)
    return jnp.concatenate([y_ssd, y_sb], axis=-1) @ out_w


def mlstm_mixer(h, in_w, i_b, f_b, norm_w, out_w):
    bsz, seq, _ = h.shape
    f32 = jnp.float32
    q, k, v, o, i_pre, f_pre = jnp.split(h @ in_w, ML_SPLITS, axis=-1)
    q = q.reshape(bsz, seq, ML_HEADS, ML_QK_DIM).astype(f32) * (ML_QK_DIM ** -0.5)
    k = k.reshape(bsz, seq, ML_HEADS, ML_QK_DIM).astype(f32)
    v = v.reshape(bsz, seq, ML_HEADS, ML_V_DIM).astype(f32)
    logi = soft_cap(i_pre.astype(f32) + i_b.astype(f32), ML_GATE_CAP)
    logf = jax.nn.log_sigmoid(soft_cap(f_pre.astype(f32) + f_b.astype(f32), ML_GATE_CAP))
    hm = mlstm_chunkwise(q, k, v, logi, logf)
    hm = rmsnorm(hm, norm_w.reshape(ML_HEADS, ML_V_DIM)).reshape(bsz, seq, ML_V).astype(h.dtype)
    return (hm * jax.nn.sigmoid(o)) @ out_w


def conv_ffn(h, up_w, conv_w, conv_b, down_w):
    u = causal_dwconv(h @ up_w, conv_w, conv_b)
    gate, val = jnp.split(u, 2, axis=-1)
    return (jax.nn.silu(gate) * val) @ down_w


def setup_inputs(seed: int = 0) -> dict:
    key = jax.random.key(seed)
    ks = jax.random.split(key, 24)
    f32 = jnp.float32
    nrm = lambda kk, shape, s: jax.random.normal(kk, shape, f32) * s
    dt0 = jnp.exp(jax.random.uniform(ks[5], (N_EVEN, SSD_HEADS), f32, math.log(1e-3), math.log(1e-1)))
    return {
        "x": nrm(ks[0], (BATCH, SEQ, D_MODEL), 1.0),
        "norm_w": 1.0 + nrm(ks[1], (DEPTH, 2, D_MODEL), 0.02),
        "hy_in_w": nrm(ks[2], (N_EVEN, D_MODEL, HY_IN_COLS), D_MODEL ** -0.5),
        "ssd_conv_w": nrm(ks[3], (N_EVEN, SSD_CONV, SSD_CONV_CH), SSD_CONV ** -0.5),
        "ssd_conv_b": nrm(ks[4], (N_EVEN, SSD_CONV_CH), 0.02),
        "ssd_dt_bias": dt0 + jnp.log(-jnp.expm1(-dt0)),
        "ssd_a_log": jnp.log(jax.random.uniform(ks[6], (N_EVEN, SSD_HEADS), f32, 1.0, 16.0)),
        "ssd_d": 1.0 + nrm(ks[7], (N_EVEN, SSD_HEADS), 0.02),
        "ssd_norm_w": 1.0 + nrm(ks[8], (N_EVEN, SSD_INNER), 0.02),
        "sb_q_norm_w": 1.0 + nrm(ks[9], (N_EVEN, SB_HEAD_DIM), 0.02),
        "sb_k_norm_w": 1.0 + nrm(ks[10], (N_EVEN, SB_HEAD_DIM), 0.02),
        "hy_out_w": nrm(ks[11], (N_EVEN, HY_MIX, D_MODEL), HY_MIX ** -0.5),
        "ml_in_w": nrm(ks[12], (N_ODD, D_MODEL, ML_IN_COLS), D_MODEL ** -0.5),
        "ml_i_b": nrm(ks[13], (N_ODD, ML_HEADS), 0.01),
        "ml_f_b": jnp.linspace(3.0, 6.0, ML_HEADS, dtype=f32) + nrm(ks[14], (N_ODD, ML_HEADS), 0.02),
        "ml_norm_w": 1.0 + nrm(ks[15], (N_ODD, ML_V), 0.02),
        "ml_out_w": nrm(ks[16], (N_ODD, ML_V, D_MODEL), ML_V ** -0.5),
        "ffn_up_w": nrm(ks[17], (DEPTH, D_MODEL, 2 * D_FF), D_MODEL ** -0.5),
        "ffn_conv_w": nrm(ks[18], (DEPTH, FFN_CONV, 2 * D_FF), FFN_CONV ** -0.5),
        "ffn_conv_b": nrm(ks[19], (DEPTH, 2 * D_FF), 0.02),
        "ffn_down_w": nrm(ks[20], (DEPTH, D_FF, D_MODEL), D_FF ** -0.5),
    }


def reference(x, norm_w, hy_in_w, ssd_conv_w, ssd_conv_b, ssd_dt_bias, ssd_a_log, ssd_d,
              ssd_norm_w, sb_q_norm_w, sb_k_norm_w, hy_out_w, ml_in_w, ml_i_b, ml_f_b,
              ml_norm_w, ml_out_w, ffn_up_w, ffn_conv_w, ffn_conv_b, ffn_down_w):
    h = x
    for layer in range(DEPTH):
        j = layer // 2
        hn = rmsnorm(h, norm_w[layer, 0])
        if layer % 2 == 0:
            h = h + ssd_stickbreak_mixer(hn, hy_in_w[j], ssd_conv_w[j], ssd_conv_b[j],
                                         ssd_dt_bias[j], ssd_a_log[j], ssd_d[j], ssd_norm_w[j],
                                         sb_q_norm_w[j], sb_k_norm_w[j], hy_out_w[j])
        else:
            h = h + mlstm_mixer(hn, ml_in_w[j], ml_i_b[j], ml_f_b[j], ml_norm_w[j], ml_out_w[j])
        h = h + conv_ffn(rmsnorm(h, norm_w[layer, 1]), ffn_up_w[layer], ffn_conv_w[layer],
                         ffn_conv_b[layer], ffn_down_w[layer])
    return h
```

```python
import functools

import jax
import jax.numpy as jnp
from jax import lax
from jax.experimental import pallas as pl
from jax.experimental.pallas import tpu as pltpu

F32 = jnp.float32
BF16 = jnp.bfloat16

D_MODEL = 1024
SSD_HEADS = 16
SSD_HEAD_DIM = 64
SSD_INNER = SSD_HEADS * SSD_HEAD_DIM
SSD_GROUPS = 2
SSD_STATE = 128
SSD_CONV = 4
SSD_BC = 2 * SSD_GROUPS * SSD_STATE
SB_HEADS = 16
SB_HEAD_DIM = 64
SB_INNER = SB_HEADS * SB_HEAD_DIM
ML_HEADS = 8
ML_QK_DIM = 64
ML_V_DIM = 128
ML_QK = ML_HEADS * ML_QK_DIM
ML_V = ML_HEADS * ML_V_DIM
ML_GATE_CAP = 15.0
D_FF = 2816
FFN_CONV = 3
RMS_EPS = 1e-6

CHUNK = 128
LANES = 128
BF16_ROWS = 16
HALO = 8
VMEM_LIMIT = 56 * 1024 * 1024

P0_Z, P0_Q, P0_K, P0_V, P0_X, P0_BC = 0, 1024, 2048, 3072, 4096, 5120
P0_COLS = 5632
P1_COLS = 3072


def _sigmoid(x):
    return 1.0 / (1.0 + jnp.exp(-x))


def _silu(x):
    return x * _sigmoid(x)


def _softplus(x):
    return jnp.maximum(x, 0.0) + jnp.log1p(jnp.exp(-jnp.abs(x)))


def _rms(x, w):
    ms = jnp.mean(x * x, axis=-1, keepdims=True)
    return x * lax.rsqrt(ms + RMS_EPS) * w


def _dot(a, b):
    return jnp.dot(a, b, preferred_element_type=F32)


def _dot_nt(a, b):
    return lax.dot_general(a, b, (((1,), (1,)), ((), ())), preferred_element_type=F32)


def _split3(v):
    hi = v.astype(BF16)
    r1 = v - hi.astype(F32)
    mid = r1.astype(BF16)
    lo = (r1 - mid.astype(F32)).astype(BF16)
    return hi, mid, lo


def _tri_cumsum(tri, v):
    hi, mid, lo = _split3(v)
    return _dot(tri, hi) + _dot(tri, mid) + _dot(tri, lo)


def _tril_incl(n):
    r = lax.broadcasted_iota(jnp.int32, (n, n), 0)
    c = lax.broadcasted_iota(jnp.int32, (n, n), 1)
    return c <= r


def _norm_proj_kernel(x_ref, nw_ref, w_ref, ws_ref, o_ref, os_ref, xn_ref):
    @pl.when(pl.program_id(1) == 0)
    def _():
        xn = _rms(x_ref[...], nw_ref[...]).astype(BF16)
        xn_ref[...] = xn
        os_ref[...] = _dot(xn, ws_ref[...])

    o_ref[...] = _dot(xn_ref[...], w_ref[...]).astype(o_ref.dtype)


def _norm_proj(x, nw, w_main, w_small, *, tm, tn):
    t, d = x.shape
    n = w_main.shape[1]
    ns = w_small.shape[1]
    return pl.pallas_call(
        _norm_proj_kernel,
        grid=(t // tm, n // tn),
        in_specs=[
            pl.BlockSpec((tm, d), lambda i, j: (i, 0)),
            pl.BlockSpec((1, d), lambda i, j: (0, 0)),
            pl.BlockSpec((d, tn), lambda i, j: (0, j)),
            pl.BlockSpec((d, ns), lambda i, j: (0, 0)),
        ],
        out_specs=[
            pl.BlockSpec((tm, tn), lambda i, j: (i, j)),
            pl.BlockSpec((tm, ns), lambda i, j: (i, 0)),
        ],
        out_shape=[jax.ShapeDtypeStruct((t, n), BF16), jax.ShapeDtypeStruct((t, ns), F32)],
        scratch_shapes=[pltpu.VMEM((tm, d), BF16)],
        compiler_params=pltpu.CompilerParams(
            dimension_semantics=("parallel", "arbitrary"), vmem_limit_bytes=VMEM_LIMIT),
        name="norm_proj",
    )(x, nw, w_main, w_small)


def _ssd_kernel(z_ref, x_ref, bc_ref, dt_ref, cw_ref, cb_ref, dtb_ref, alog_ref, dsk_ref, nw_ref,
                o_ref, cbuf_ref, state_ref):
    c = pl.program_id(1)
    l = CHUNK
    p = SSD_HEAD_DIM
    hpg = SSD_HEADS // SSD_GROUPS

    @pl.when(c == 0)
    def _():
        cbuf_ref[0:HALO, :] = jnp.zeros((HALO, SSD_INNER + SSD_BC), F32)
        state_ref[...] = jnp.zeros(state_ref.shape, F32)

    cbuf_ref[HALO:HALO + l, 0:SSD_INNER] = x_ref[...].astype(F32)
    cbuf_ref[HALO:HALO + l, SSD_INNER:] = bc_ref[...].astype(F32)
    conv = cb_ref[...]
    for k in range(SSD_CONV):
        off = HALO - (SSD_CONV - 1) + k
        conv = conv + cw_ref[k:k + 1, :] * cbuf_ref[off:off + l, :]
    cbuf_ref[0:HALO, :] = cbuf_ref[l:l + HALO, :]
    xbc = _silu(conv)
    xs = xbc[:, 0:SSD_INNER]

    dt = _softplus(dt_ref[...] + dtb_ref[...])
    da = dt * (-jnp.exp(alog_ref[...]))
    tri = _tril_incl(l)
    acs = _tri_cumsum(tri.astype(BF16), da)
    acs_t = acs.T
    last = acs[l - 1:l, :]
    dend = jnp.exp(last - acs)
    cdec = jnp.exp(last)
    eacs = jnp.exp(acs)

    ys = []
    for g in range(SSD_GROUPS):
        bm = xbc[:, SSD_INNER + g * SSD_STATE:SSD_INNER + (g + 1) * SSD_STATE]
        cm = xbc[:, SSD_INNER + (SSD_GROUPS + g) * SSD_STATE:SSD_INNER + (SSD_GROUPS + g + 1) * SSD_STATE]
        bm_b = bm.astype(BF16)
        cm_b = cm.astype(BF16)
        cb = _dot_nt(cm_b, bm_b)
        bm_t = bm.T.astype(BF16)
        for r in range(hpg):
            h = g * hpg + r
            seg = acs[:, h:h + 1] - acs_t[h:h + 1, :]
            decay = jnp.exp(jnp.where(tri, seg, -jnp.inf))
            m = (cb * decay).astype(BF16)
            xd = xs[:, h * p:(h + 1) * p] * dt[:, h:h + 1]
            st = state_ref[h]
            lhs = jnp.concatenate([m, (cm * eacs[:, h:h + 1]).astype(BF16)], axis=1)
            rhs = jnp.concatenate([xd.astype(BF16), st.astype(BF16)], axis=0)
            ys.append(_dot(lhs, rhs))
            new = _dot(bm_t, (xd * dend[:, h:h + 1]).astype(BF16))
            state_ref[h] = cdec[:, h:h + 1] * st + new

    y = jnp.concatenate(ys, axis=1) + xs * dsk_ref[...]
    y = y * _silu(z_ref[...].astype(F32))
    o_ref[...] = _rms(y, nw_ref[...]).astype(o_ref.dtype)


def _ssd(p0, dt_raw, conv_w, conv_b, dt_bias, a_log, d_skip, norm_w, *, bsz, seq):
    nc = seq // CHUNK
    l = CHUNK
    row = lambda b, c: b * nc + c
    full = lambda shape: pl.BlockSpec(shape, lambda b, c: (0, 0))
    return pl.pallas_call(
        _ssd_kernel,
        grid=(bsz, nc),
        in_specs=[
            pl.BlockSpec((l, SSD_INNER), lambda b, c: (row(b, c), P0_Z // SSD_INNER)),
            pl.BlockSpec((l, SSD_INNER), lambda b, c: (row(b, c), P0_X // SSD_INNER)),
            pl.BlockSpec((l, SSD_BC), lambda b, c: (row(b, c), P0_BC // SSD_BC)),
            pl.BlockSpec((l, LANES), lambda b, c: (row(b, c), 0)),
            full((SSD_CONV, SSD_INNER + SSD_BC)),
            full((1, SSD_INNER + SSD_BC)),
            full((1, LANES)),
            full((1, LANES)),
            full((1, SSD_INNER)),
            full((1, SSD_INNER)),
        ],
        out_specs=pl.BlockSpec((l, SSD_INNER), lambda b, c: (row(b, c), 0)),
        out_shape=jax.ShapeDtypeStruct((bsz * seq, SSD_INNER), BF16),
        scratch_shapes=[
            pltpu.VMEM((l + HALO, SSD_INNER + SSD_BC), F32),
            pltpu.VMEM((SSD_HEADS, SSD_STATE, SSD_HEAD_DIM), F32),
        ],
        compiler_params=pltpu.CompilerParams(
            dimension_semantics=("parallel", "arbitrary"), vmem_limit_bytes=VMEM_LIMIT),
        name="ssd_scan",
    )(p0, p0, p0, dt_raw, conv_w, conv_b, dt_bias, a_log, d_skip, norm_w)


def _half_rms(x, w):
    lane = lax.broadcasted_iota(jnp.int32, x.shape, 1)
    lo = lane < SB_HEAD_DIM
    x2 = x * x
    s0 = jnp.sum(jnp.where(lo, x2, 0.0), axis=-1, keepdims=True)
    s1 = jnp.sum(jnp.where(lo, 0.0, x2), axis=-1, keepdims=True)
    ms = jnp.where(lo, s0, s1) * (1.0 / SB_HEAD_DIM)
    return x * lax.rsqrt(ms + RMS_EPS) * w


def _sb_kernel(q_ref, k_ref, v_ref, qw_ref, kw_ref, w2_ref, o_ref, kt_ref, *, nkb):
    qi = pl.program_id(2)
    blk = CHUNK
    dh = SB_HEAD_DIM

    @pl.when(qi == 0)
    def _():
        for kb in range(nkb):
            kn = _half_rms(k_ref[kb * blk:(kb + 1) * blk, :].astype(F32), kw_ref[...])
            kt_ref[kb] = kn.T.astype(BF16)

    qn = (_half_rms(q_ref[...].astype(F32), qw_ref[...]) * (dh ** -0.5)).astype(BF16)
    qh = [qn[:, hh * dh:(hh + 1) * dh] for hh in range(2)]
    w2 = w2_ref[...]
    row = lax.broadcasted_iota(jnp.int32, (blk, blk), 0)
    col = lax.broadcasted_iota(jnp.int32, (blk, blk), 1)
    causal = col < row

    def block(kb, carry, diag):
        kt = kt_ref[kb]
        start = pl.multiple_of(kb * blk, blk)
        vb = v_ref[pl.ds(start, blk), :]
        out = []
        for hh in range(2):
            rem_c, acc = carry[hh]
            z = _dot(qh[hh], kt[hh * dh:(hh + 1) * dh, :])
            sp = _softplus(z)
            spm = jnp.where(causal, sp, 0.0) if diag else sp
            hi = spm.astype(BF16)
            lo = (spm - hi.astype(F32)).astype(BF16)
            r = _dot(jnp.concatenate([hi, lo], axis=1), w2)
            logw = (z - sp) + r[:, 0:blk] + rem_c
            w = jnp.exp(logw)
            if diag:
                w = jnp.where(causal, w, 0.0)
            acc = acc + _dot(w.astype(BF16), vb[:, hh * dh:(hh + 1) * dh])
            out.append((rem_c + r[:, blk:2 * blk], acc))
        return tuple(out)

    init = tuple((jnp.zeros((blk, blk), F32), jnp.zeros((blk, dh), F32)) for _ in range(2))
    carry = block(qi, init, True)
    carry = lax.fori_loop(0, qi, lambda i, cr: block(qi - 1 - i, cr, False), carry)
    o_ref[...] = jnp.concatenate([carry[0][1], carry[1][1]], axis=1).astype(o_ref.dtype)


def _stick_breaking(p0, q_norm_w, k_norm_w, *, bsz, seq):
    blk = CHUNK
    nq = seq // blk
    hp = SB_HEADS // 2
    r = jnp.arange(2 * blk)[:, None] % blk
    cidx = jnp.arange(2 * blk)[None, :]
    w2 = jnp.where((cidx >= blk) | (r > cidx), -1.0, 0.0).astype(BF16)
    return pl.pallas_call(
        functools.partial(_sb_kernel, nkb=nq),
        grid=(bsz, hp, nq),
        in_specs=[
            pl.BlockSpec((blk, LANES), lambda b, h, i: (b * nq + i, P0_Q // LANES + h)),
            pl.BlockSpec((seq, LANES), lambda b, h, i: (b, P0_K // LANES + h)),
            pl.BlockSpec((seq, LANES), lambda b, h, i: (b, P0_V // LANES + h)),
            pl.BlockSpec((1, LANES), lambda b, h, i: (0, 0)),
            pl.BlockSpec((1, LANES), lambda b, h, i: (0, 0)),
            pl.BlockSpec((2 * blk, 2 * blk), lambda b, h, i: (0, 0)),
        ],
        out_specs=pl.BlockSpec((blk, LANES), lambda b, h, i: (b * nq + i, h)),
        out_shape=jax.ShapeDtypeStruct((bsz * seq, SB_INNER), BF16),
        scratch_shapes=[pltpu.VMEM((nq, LANES, blk), BF16)],
        compiler_params=pltpu.CompilerParams(
            dimension_semantics=("parallel", "parallel", "arbitrary"), vmem_limit_bytes=VMEM_LIMIT),
        name="stick_breaking",
    )(p0, p0, p0, q_norm_w, k_norm_w, w2)


def _out_proj_kernel(*refs, n_in):
    a_refs = refs[:n_in]
    w_refs = refs[n_in:2 * n_in]
    x_ref = refs[2 * n_in]
    o_ref = refs[2 * n_in + 1]
    acc = x_ref[...]
    for a_ref, w_ref in zip(a_refs, w_refs):
        acc = acc + _dot(a_ref[...], w_ref[...])
    o_ref[...] = acc


def _out_proj(acts, weights, x, *, tm):
    t, d = x.shape
    n_in = len(acts)
    in_specs = [pl.BlockSpec((tm, a.shape[1]), lambda i: (i, 0)) for a in acts]
    in_specs += [pl.BlockSpec(w.shape, lambda i: (0, 0)) for w in weights]
    in_specs += [pl.BlockSpec((tm, d), lambda i: (i, 0))]
    return pl.pallas_call(
        functools.partial(_out_proj_kernel, n_in=n_in),
        grid=(t // tm,),
        in_specs=in_specs,
        out_specs=pl.BlockSpec((tm, d), lambda i: (i, 0)),
        out_shape=jax.ShapeDtypeStruct((t, d), F32),
        compiler_params=pltpu.CompilerParams(
            dimension_semantics=("parallel",), vmem_limit_bytes=VMEM_LIMIT),
        name="out_proj",
    )(*acts, *weights, x)


def _ffn_kernel(h_ref, halo_ref, nw_ref, upw_ref, cw_ref, cb_ref, dw_ref, o_ref, *, tm, seq, cw):
    i = pl.program_id(0)
    hr = BF16_ROWS
    h = h_ref[...]
    keep = jnp.where(lax.rem(i * tm, seq) == 0, 0.0, 1.0)
    hn_ext = jnp.concatenate(
        [_rms(halo_ref[...], nw_ref[...]) * keep, _rms(h, nw_ref[...])], axis=0).astype(BF16)
    acc = h
    for c in range(D_FF // cw):
        ys = []
        for base in (0, D_FF):
            sl = slice(base + c * cw, base + (c + 1) * cw)
            u = _dot(hn_ext, upw_ref[:, sl])
            y = cb_ref[:, sl]
            for k in range(FFN_CONV):
                off = hr - (FFN_CONV - 1) + k
                y = y + cw_ref[k:k + 1, sl] * u[off:off + tm, :]
            ys.append(y)
        act = (_silu(ys[0]) * ys[1]).astype(BF16)
        acc = acc + _dot(act, dw_ref[c * cw:(c + 1) * cw, :])
    o_ref[...] = acc


def _conv_ffn(h, nw, up_w, conv_w, conv_b, down_w, *, seq, tm, cw):
    t, d = h.shape
    hr = BF16_ROWS
    const = lambda shape: pl.BlockSpec(shape, lambda i: (0, 0))
    return pl.pallas_call(
        functools.partial(_ffn_kernel, tm=tm, seq=seq, cw=cw),
        grid=(t // tm,),
        in_specs=[
            pl.BlockSpec((tm, d), lambda i: (i, 0)),
            pl.BlockSpec((hr, d), lambda i: (jnp.maximum(i * (tm // hr) - 1, 0), 0)),
            const((1, d)),
            const(up_w.shape),
            const(conv_w.shape),
            const(conv_b.shape),
            const(down_w.shape),
        ],
        out_specs=pl.BlockSpec((tm, d), lambda i: (i, 0)),
        out_shape=jax.ShapeDtypeStruct((t, d), F32),
        compiler_params=pltpu.CompilerParams(
            dimension_semantics=("parallel",), vmem_limit_bytes=VMEM_LIMIT),
        name="conv_ffn",
    )(h, h, nw, up_w, conv_w, conv_b, down_w)


def _mlstm_kernel(q_ref, k_ref, v_ref, og_ref, ig_ref, fg_ref, ib_ref, fb_ref, nw_ref,
                  o_ref, cn_ref, m_ref):
    c = pl.program_id(1)
    l = CHUNK
    kd = ML_QK_DIM
    vd = ML_V_DIM

    @pl.when(c == 0)
    def _():
        cn_ref[...] = jnp.zeros(cn_ref.shape, F32)
        m_ref[...] = jnp.zeros(m_ref.shape, F32)

    cap = lambda x: ML_GATE_CAP * jnp.tanh(x * (1.0 / ML_GATE_CAP))
    logi = cap(ig_ref[...] + ib_ref[...])
    fc = cap(fg_ref[...] + fb_ref[...])
    logf = jnp.minimum(fc, 0.0) - jnp.log1p(jnp.exp(-jnp.abs(fc)))
    tri = _tril_incl(l)
    bcs = _tri_cumsum(tri.astype(BF16), logf)
    bcs_t = bcs.T
    logi_t = logi.T
    glast = bcs[l - 1:l, :]
    m_all = m_ref[...]

    q = q_ref[...]
    k = k_ref[...]
    k_t = k.astype(F32).T.astype(BF16)
    for h in range(ML_HEADS):
        qh = (q[:, h * kd:(h + 1) * kd].astype(F32) * (kd ** -0.5)).astype(BF16)
        kh = k[:, h * kd:(h + 1) * kd]
        vh = v_ref[:, h * vd:(h + 1) * vd]
        bcol = bcs[:, h:h + 1]
        m_prev = m_all[:, h:h + 1]
        dm = jnp.where(tri, bcol - bcs_t[h:h + 1, :] + logi_t[h:h + 1, :], -jnp.inf)
        inter = bcol + m_prev
        m_t = jnp.maximum(inter, jnp.max(dm, axis=-1, keepdims=True))
        sw = jnp.exp(dm - m_t) * _dot_nt(qh, kh)
        iw = jnp.exp(inter - m_t)
        cn = cn_ref[h]
        qc = _dot(qh, cn.astype(BF16))
        num = _dot(sw.astype(BF16), vh) + qc[:, 0:vd] * iw
        den = jnp.sum(sw, axis=-1, keepdims=True) + qc[:, vd:vd + 1] * iw
        hm = num / jnp.maximum(jnp.abs(den), jnp.exp(-m_t))
        hm = _rms(hm, nw_ref[:, h * vd:(h + 1) * vd])
        og = og_ref[:, h * vd:(h + 1) * vd].astype(F32)
        o_ref[:, h * vd:(h + 1) * vd] = (hm * _sigmoid(og)).astype(o_ref.dtype)

        g = glast[:, h:h + 1]
        w_end = g - bcol + logi[:, h:h + 1]
        m_loc = jnp.max(w_end, axis=0, keepdims=True)
        e_end = jnp.exp(w_end - m_loc)
        xe = jnp.concatenate(
            [vh.astype(F32) * e_end, jnp.broadcast_to(e_end, (l, vd))], axis=1).astype(BF16)
        loc = _dot(k_t[h * kd:(h + 1) * kd, :], xe)
        m_new = jnp.maximum(g + m_prev, m_loc)
        cn_ref[h] = jnp.exp(g + m_prev - m_new) * cn + jnp.exp(m_loc - m_new) * loc
        m_ref[:, h:h + 1] = m_new


def _mlstm(p1, gates, i_b, f_b, norm_w, *, bsz, seq):
    nc = seq // CHUNK
    l = CHUNK
    row = lambda b, c: b * nc + c
    full = lambda shape: pl.BlockSpec(shape, lambda b, c: (0, 0))
    return pl.pallas_call(
        _mlstm_kernel,
        grid=(bsz, nc),
        in_specs=[
            pl.BlockSpec((l, ML_QK), lambda b, c: (row(b, c), 0)),
            pl.BlockSpec((l, ML_QK), lambda b, c: (row(b, c), 1)),
            pl.BlockSpec((l, ML_V), lambda b, c: (row(b, c), 1)),
            pl.BlockSpec((l, ML_V), lambda b, c: (row(b, c), 2)),
            pl.BlockSpec((l, LANES), lambda b, c: (row(b, c), 0)),
            pl.BlockSpec((l, LANES), lambda b, c: (row(b, c), 1)),
            full((1, LANES)),
            full((1, LANES)),
            full((1, ML_V)),
        ],
        out_specs=pl.BlockSpec((l, ML_V), lambda b, c: (row(b, c), 0)),
        out_shape=jax.ShapeDtypeStruct((bsz * seq, ML_V), BF16),
        scratch_shapes=[
            pltpu.VMEM((ML_HEADS, ML_QK_DIM, 2 * ML_V_DIM), F32),
            pltpu.VMEM((1, LANES), F32),
        ],
        compiler_params=pltpu.CompilerParams(
            dimension_semantics=("parallel", "arbitrary"), vmem_limit_bytes=VMEM_LIMIT),
        name="mlstm_scan",
    )(p1, p1, p1, p1, gates, gates, i_b, f_b, norm_w)


def _pad_cols(w, n):
    return jnp.pad(w, ((0, 0), (0, n - w.shape[1])))


def _row(v, n=None):
    v = v.reshape(1, -1).astype(F32)
    return v if n is None else _pad_cols(v, n)


def kernel(x, norm_w, hy_in_w, ssd_conv_w, ssd_conv_b, ssd_dt_bias, ssd_a_log, ssd_d, ssd_norm_w,
           sb_q_norm_w, sb_k_norm_w, hy_out_w, ml_in_w, ml_i_b, ml_f_b, ml_norm_w, ml_out_w,
           ffn_up_w, ffn_conv_w, ffn_conv_b, ffn_down_w):
    bsz, seq, d = x.shape
    t = bsz * seq
    tm = min(512, seq)
    h = x.reshape(t, d)

    w = hy_in_w[0]
    o_xbc = SSD_INNER
    o_dt = o_xbc + SSD_INNER + SSD_BC
    o_q = o_dt + SSD_HEADS
    w_z = w[:, 0:SSD_INNER]
    w_x = w[:, o_xbc:o_xbc + SSD_INNER]
    w_bc = w[:, o_xbc + SSD_INNER:o_dt]
    w_dt = w[:, o_dt:o_q]
    w_qkv = w[:, o_q:]
    w_main = jnp.concatenate([w_z, w_qkv, w_x, w_bc], axis=1).astype(BF16)
    w_small = _pad_cols(w_dt, LANES).astype(BF16)
    p0, dt_raw = _norm_proj(h, _row(norm_w[0, 0]), w_main, w_small, tm=tm, tn=512)

    y_ssd = _ssd(p0, dt_raw, ssd_conv_w[0], _row(ssd_conv_b[0]), _row(ssd_dt_bias[0], LANES),
                 _row(ssd_a_log[0], LANES), _row(jnp.repeat(ssd_d[0], SSD_HEAD_DIM)),
                 _row(ssd_norm_w[0]), bsz=bsz, seq=seq)
    y_sb = _stick_breaking(p0, _row(jnp.tile(sb_q_norm_w[0], 2)), _row(jnp.tile(sb_k_norm_w[0], 2)),
                           bsz=bsz, seq=seq)
    w_out = hy_out_w[0].astype(BF16)
    h = _out_proj([y_ssd, y_sb], [w_out[0:SSD_INNER], w_out[SSD_INNER:]], h, tm=tm)
    h = _conv_ffn(h, _row(norm_w[0, 1]), ffn_up_w[0].astype(BF16), ffn_conv_w[0], _row(ffn_conv_b[0]),
                  ffn_down_w[0].astype(BF16), seq=seq, tm=tm, cw=D_FF // 2)

    w = ml_in_w[0]
    o_g = 2 * ML_QK + 2 * ML_V
    w_main = w[:, 0:o_g].astype(BF16)
    w_small = jnp.concatenate(
        [_pad_cols(w[:, o_g:o_g + ML_HEADS], LANES), _pad_cols(w[:, o_g + ML_HEADS:], LANES)],
        axis=1).astype(BF16)
    p1, gates = _norm_proj(h, _row(norm_w[1, 0]), w_main, w_small, tm=tm, tn=512)
    y_ml = _mlstm(p1, gates, _row(ml_i_b[0], LANES), _row(ml_f_b[0], LANES), _row(ml_norm_w[0]),
                  bsz=bsz, seq=seq)
    h = _out_proj([y_ml], [ml_out_w[0].astype(BF16)], h, tm=tm)
    h = _conv_ffn(h, _row(norm_w[1, 1]), ffn_up_w[1].astype(BF16), ffn_conv_w[1], _row(ffn_conv_b[1]),
                  ffn_down_w[1].astype(BF16), seq=seq, tm=tm, cw=D_FF // 2)
    return h.reshape(bsz, seq, d)
```

```python
import functools

import jax
import jax.numpy as jnp
from jax import lax
from jax.experimental import pallas as pl
from jax.experimental.pallas import tpu as pltpu

F32 = jnp.float32
BF16 = jnp.bfloat16

D_MODEL = 1024
SSD_HEADS = 16
SSD_HEAD_DIM = 64
SSD_INNER = SSD_HEADS * SSD_HEAD_DIM
SSD_GROUPS = 2
SSD_STATE = 128
SSD_CONV = 4
SSD_BC = 2 * SSD_GROUPS * SSD_STATE
SB_HEADS = 16
SB_HEAD_DIM = 64
SB_INNER = SB_HEADS * SB_HEAD_DIM
ML_HEADS = 8
ML_QK_DIM = 64
ML_V_DIM = 128
ML_QK = ML_HEADS * ML_QK_DIM
ML_V = ML_HEADS * ML_V_DIM
ML_GATE_CAP = 15.0
D_FF = 2816
FFN_CONV = 3
RMS_EPS = 1e-6

CHUNK = 128
SB_TILE = 256
LOG2E = 1.4426950408889634
LANES = 128
BF16_ROWS = 16
HALO = 8
VMEM_LIMIT = 56 * 1024 * 1024

P0_Z, P0_Q, P0_K, P0_V, P0_X, P0_BC = 0, 1024, 2048, 3072, 4096, 5120
P0_COLS = 5632
P1_COLS = 3072


def _sigmoid(x):
    return 1.0 / (1.0 + jnp.exp(-x))


def _silu(x):
    return x * _sigmoid(x)


def _softplus(x):
    return jnp.maximum(x, 0.0) + jnp.log1p(jnp.exp(-jnp.abs(x)))


def _rms(x, w):
    ms = jnp.mean(x * x, axis=-1, keepdims=True)
    return x * lax.rsqrt(ms + RMS_EPS) * w


def _dot(a, b):
    return jnp.dot(a, b, preferred_element_type=F32)


def _dot_nt(a, b):
    return lax.dot_general(a, b, (((1,), (1,)), ((), ())), preferred_element_type=F32)


def _split3(v):
    hi = v.astype(BF16)
    r1 = v - hi.astype(F32)
    mid = r1.astype(BF16)
    lo = (r1 - mid.astype(F32)).astype(BF16)
    return hi, mid, lo


def _tri_cumsum(tri, v):
    hi, mid, lo = _split3(v)
    return _dot(tri, hi) + _dot(tri, mid) + _dot(tri, lo)


def _tril_incl(n):
    r = lax.broadcasted_iota(jnp.int32, (n, n), 0)
    c = lax.broadcasted_iota(jnp.int32, (n, n), 1)
    return c <= r


def _norm_proj_kernel(x_ref, nw_ref, w_ref, ws_ref, o_ref, os_ref, xn_ref):
    @pl.when(pl.program_id(1) == 0)
    def _():
        xn = _rms(x_ref[...], nw_ref[...]).astype(BF16)
        xn_ref[...] = xn
        os_ref[...] = _dot(xn, ws_ref[...])

    o_ref[...] = _dot(xn_ref[...], w_ref[...]).astype(o_ref.dtype)


def _norm_proj(x, nw, w_main, w_small, *, tm, tn):
    t, d = x.shape
    n = w_main.shape[1]
    ns = w_small.shape[1]
    return pl.pallas_call(
        _norm_proj_kernel,
        grid=(t // tm, n // tn),
        in_specs=[
            pl.BlockSpec((tm, d), lambda i, j: (i, 0)),
            pl.BlockSpec((1, d), lambda i, j: (0, 0)),
            pl.BlockSpec((d, tn), lambda i, j: (0, j)),
            pl.BlockSpec((d, ns), lambda i, j: (0, 0)),
        ],
        out_specs=[
            pl.BlockSpec((tm, tn), lambda i, j: (i, j)),
            pl.BlockSpec((tm, ns), lambda i, j: (i, 0)),
        ],
        out_shape=[jax.ShapeDtypeStruct((t, n), BF16), jax.ShapeDtypeStruct((t, ns), F32)],
        scratch_shapes=[pltpu.VMEM((tm, d), BF16)],
        compiler_params=pltpu.CompilerParams(
            dimension_semantics=("parallel", "arbitrary"), vmem_limit_bytes=VMEM_LIMIT),
        name="norm_proj",
    )(x, nw, w_main, w_small)


def _ssd_kernel(z_ref, x_ref, bc_ref, dt_ref, cw_ref, cb_ref, dtb_ref, alog_ref, dsk_ref, nw_ref,
                o_ref, cbuf_ref, state_ref):
    c = pl.program_id(1)
    l = CHUNK
    p = SSD_HEAD_DIM
    hpg = SSD_HEADS // SSD_GROUPS

    @pl.when(c == 0)
    def _():
        cbuf_ref[0:HALO, :] = jnp.zeros((HALO, SSD_INNER + SSD_BC), F32)
        state_ref[...] = jnp.zeros(state_ref.shape, F32)

    cbuf_ref[HALO:HALO + l, 0:SSD_INNER] = x_ref[...].astype(F32)
    cbuf_ref[HALO:HALO + l, SSD_INNER:] = bc_ref[...].astype(F32)
    conv = cb_ref[...]
    for k in range(SSD_CONV):
        off = HALO - (SSD_CONV - 1) + k
        conv = conv + cw_ref[k:k + 1, :] * cbuf_ref[off:off + l, :]
    cbuf_ref[0:HALO, :] = cbuf_ref[l:l + HALO, :]
    xbc = _silu(conv)
    xs = xbc[:, 0:SSD_INNER]

    dt = _softplus(dt_ref[...] + dtb_ref[...])
    da = dt * (-jnp.exp(alog_ref[...]))
    tri = _tril_incl(l)
    acs = _tri_cumsum(tri.astype(BF16), da)
    acs_t = acs.T
    last = acs[l - 1:l, :]
    dend = jnp.exp(last - acs)
    cdec = jnp.exp(last)
    eacs = jnp.exp(acs)

    ys = []
    for g in range(SSD_GROUPS):
        bm = xbc[:, SSD_INNER + g * SSD_STATE:SSD_INNER + (g + 1) * SSD_STATE]
        cm = xbc[:, SSD_INNER + (SSD_GROUPS + g) * SSD_STATE:SSD_INNER + (SSD_GROUPS + g + 1) * SSD_STATE]
        bm_b = bm.astype(BF16)
        cm_b = cm.astype(BF16)
        cb = _dot_nt(cm_b, bm_b)
        bm_t = bm.T.astype(BF16)
        for r in range(hpg):
            h = g * hpg + r
            seg = acs[:, h:h + 1] - acs_t[h:h + 1, :]
            decay = jnp.exp(jnp.where(tri, seg, -jnp.inf))
            m = (cb * decay).astype(BF16)
            xd = xs[:, h * p:(h + 1) * p] * dt[:, h:h + 1]
            st = state_ref[h]
            lhs = jnp.concatenate([m, (cm * eacs[:, h:h + 1]).astype(BF16)], axis=1)
            rhs = jnp.concatenate([xd.astype(BF16), st.astype(BF16)], axis=0)
            ys.append(_dot(lhs, rhs))
            new = _dot(bm_t, (xd * dend[:, h:h + 1]).astype(BF16))
            state_ref[h] = cdec[:, h:h + 1] * st + new

    y = jnp.concatenate(ys, axis=1) + xs * dsk_ref[...]
    y = y * _silu(z_ref[...].astype(F32))
    o_ref[...] = _rms(y, nw_ref[...]).astype(o_ref.dtype)


def _ssd(p0, dt_raw, conv_w, conv_b, dt_bias, a_log, d_skip, norm_w, *, bsz, seq):
    nc = seq // CHUNK
    l = CHUNK
    row = lambda b, c: b * nc + c
    full = lambda shape: pl.BlockSpec(shape, lambda b, c: (0, 0))
    return pl.pallas_call(
        _ssd_kernel,
        grid=(bsz, nc),
        in_specs=[
            pl.BlockSpec((l, SSD_INNER), lambda b, c: (row(b, c), P0_Z // SSD_INNER)),
            pl.BlockSpec((l, SSD_INNER), lambda b, c: (row(b, c), P0_X // SSD_INNER)),
            pl.BlockSpec((l, SSD_BC), lambda b, c: (row(b, c), P0_BC // SSD_BC)),
            pl.BlockSpec((l, LANES), lambda b, c: (row(b, c), 0)),
            full((SSD_CONV, SSD_INNER + SSD_BC)),
            full((1, SSD_INNER + SSD_BC)),
            full((1, LANES)),
            full((1, LANES)),
            full((1, SSD_INNER)),
            full((1, SSD_INNER)),
        ],
        out_specs=pl.BlockSpec((l, SSD_INNER), lambda b, c: (row(b, c), 0)),
        out_shape=jax.ShapeDtypeStruct((bsz * seq, SSD_INNER), BF16),
        scratch_shapes=[
            pltpu.VMEM((l + HALO, SSD_INNER + SSD_BC), F32),
            pltpu.VMEM((SSD_HEADS, SSD_STATE, SSD_HEAD_DIM), F32),
        ],
        compiler_params=pltpu.CompilerParams(
            dimension_semantics=("parallel", "arbitrary"), vmem_limit_bytes=VMEM_LIMIT),
        name="ssd_scan",
    )(p0, p0, p0, dt_raw, conv_w, conv_b, dt_bias, a_log, d_skip, norm_w)


def _half_rms(x, w):
    lane = lax.broadcasted_iota(jnp.int32, x.shape, 1)
    lo = lane < SB_HEAD_DIM
    x2 = x * x
    s0 = jnp.sum(jnp.where(lo, x2, 0.0), axis=-1, keepdims=True)
    s1 = jnp.sum(jnp.where(lo, 0.0, x2), axis=-1, keepdims=True)
    ms = jnp.where(lo, s0, s1) * (1.0 / SB_HEAD_DIM)
    return x * lax.rsqrt(ms + RMS_EPS) * w


def _sb_kernel(q_ref, k_ref, v_ref, qw_ref, kw_ref, w2_ref, o_ref, kt_ref, vz_ref, carry_ref, acc_ref,
               *, nkb):
    qi = pl.program_id(2)
    tq = SB_TILE
    sub = CHUNK
    dh = SB_HEAD_DIM

    @pl.when(qi == 0)
    def _():
        lane = lax.broadcasted_iota(jnp.int32, (tq, LANES), 1)
        for kb in range(nkb):
            rows = slice(kb * tq, (kb + 1) * tq)
            kn = _half_rms(k_ref[rows, :].astype(F32), kw_ref[...])
            kt_ref[kb] = kn.T.astype(BF16)
            v = v_ref[rows, :].astype(F32)
            vz_ref[kb, 0:tq, :] = jnp.where(lane < dh, v, 0.0).astype(BF16)
            vz_ref[kb, tq:2 * tq, :] = jnp.where(lane < dh, 0.0, v).astype(BF16)

    qn = (_half_rms(q_ref[...].astype(F32), qw_ref[...]) * (dh ** -0.5 * LOG2E)).astype(BF16)
    qh = [qn[:, hh * dh:(hh + 1) * dh] for hh in range(2)]
    w2 = w2_ref[...]
    row = lax.broadcasted_iota(jnp.int32, (tq, tq), 0)
    col = lax.broadcasted_iota(jnp.int32, (tq, tq), 1)
    causal = col < row

    def tile(kb, diag):
        kt = kt_ref[kb]
        ws = []
        for hh in range(2):
            z = _dot(qh[hh], kt[hh * dh:(hh + 1) * dh, :])
            sp = jnp.maximum(z, 0.0) + jnp.log2(1.0 + jnp.exp2(-jnp.abs(z)))
            spm = jnp.where(causal, sp, 0.0) if diag else sp
            hi = spm.astype(BF16)
            lo = (spm - hi.astype(F32)).astype(BF16)
            r1 = _dot(jnp.concatenate([hi[:, sub:], lo[:, sub:]], axis=1), w2)
            r0 = _dot(jnp.concatenate([hi[:, :sub], lo[:, :sub]], axis=1), w2)
            if diag:
                c1 = r1[:, sub:]
                rem = jnp.concatenate([r0[:, :sub] + c1, r1[:, :sub]], axis=1)
            else:
                cin = carry_ref[hh]
                c1 = cin + r1[:, sub:]
                rem = jnp.concatenate([r0[:, :sub] + c1, r1[:, :sub] + cin], axis=1)
            carry_ref[hh] = c1 + r0[:, sub:]
            w = jnp.exp2((z - sp) + rem)
            if diag:
                w = jnp.where(causal, w, 0.0)
            ws.append(w.astype(BF16))
        pv = _dot(jnp.concatenate(ws, axis=1), vz_ref[kb])
        if diag:
            acc_ref[...] = pv
        else:
            acc_ref[...] += pv

    tile(qi, True)

    def body(i, c):
        tile(qi - 1 - i, False)
        return c

    lax.fori_loop(0, qi, body, 0)
    o_ref[...] = acc_ref[...].astype(o_ref.dtype)


def _stick_breaking(p0, q_norm_w, k_norm_w, *, bsz, seq):
    tq = SB_TILE
    sub = CHUNK
    nq = seq // tq
    hp = SB_HEADS // 2
    r = jnp.arange(2 * sub)[:, None] % sub
    cidx = jnp.arange(2 * sub)[None, :]
    w2 = jnp.where((cidx >= sub) | (r > cidx), -1.0, 0.0).astype(BF16)
    return pl.pallas_call(
        functools.partial(_sb_kernel, nkb=nq),
        grid=(bsz, hp, nq),
        in_specs=[
            pl.BlockSpec((tq, LANES), lambda b, h, i: (b * nq + i, P0_Q // LANES + h)),
            pl.BlockSpec((seq, LANES), lambda b, h, i: (b, P0_K // LANES + h)),
            pl.BlockSpec((seq, LANES), lambda b, h, i: (b, P0_V // LANES + h)),
            pl.BlockSpec((1, LANES), lambda b, h, i: (0, 0)),
            pl.BlockSpec((1, LANES), lambda b, h, i: (0, 0)),
            pl.BlockSpec((2 * sub, 2 * sub), lambda b, h, i: (0, 0)),
        ],
        out_specs=pl.BlockSpec((tq, LANES), lambda b, h, i: (b * nq + i, h)),
        out_shape=jax.ShapeDtypeStruct((bsz * seq, SB_INNER), BF16),
        scratch_shapes=[
            pltpu.VMEM((nq, LANES, tq), BF16),
            pltpu.VMEM((nq, 2 * tq, LANES), BF16),
            pltpu.VMEM((2, tq, sub), F32),
            pltpu.VMEM((tq, LANES), F32),
        ],
        compiler_params=pltpu.CompilerParams(
            dimension_semantics=("parallel", "parallel", "arbitrary"), vmem_limit_bytes=VMEM_LIMIT),
        name="stick_breaking",
    )(p0, p0, p0, q_norm_w, k_norm_w, w2)


def _out_proj_kernel(*refs, n_in):
    a_refs = refs[:n_in]
    w_refs = refs[n_in:2 * n_in]
    x_ref = refs[2 * n_in]
    o_ref = refs[2 * n_in + 1]
    acc = x_ref[...]
    for a_ref, w_ref in zip(a_refs, w_refs):
        acc = acc + _dot(a_ref[...], w_ref[...])
    o_ref[...] = acc


def _out_proj(acts, weights, x, *, tm):
    t, d = x.shape
    n_in = len(acts)
    in_specs = [pl.BlockSpec((tm, a.shape[1]), lambda i: (i, 0)) for a in acts]
    in_specs += [pl.BlockSpec(w.shape, lambda i: (0, 0)) for w in weights]
    in_specs += [pl.BlockSpec((tm, d), lambda i: (i, 0))]
    return pl.pallas_call(
        functools.partial(_out_proj_kernel, n_in=n_in),
        grid=(t // tm,),
        in_specs=in_specs,
        out_specs=pl.BlockSpec((tm, d), lambda i: (i, 0)),
        out_shape=jax.ShapeDtypeStruct((t, d), F32),
        compiler_params=pltpu.CompilerParams(
            dimension_semantics=("parallel",), vmem_limit_bytes=VMEM_LIMIT),
        name="out_proj",
    )(*acts, *weights, x)


def _ffn_kernel(h_ref, halo_ref, nw_ref, upw_ref, cw_ref, cb_ref, dw_ref, o_ref, *, tm, seq, cw):
    i = pl.program_id(0)
    hr = BF16_ROWS
    h = h_ref[...]
    keep = jnp.where(lax.rem(i * tm, seq) == 0, 0.0, 1.0)
    hn_ext = jnp.concatenate(
        [_rms(halo_ref[...], nw_ref[...]) * keep, _rms(h, nw_ref[...])], axis=0).astype(BF16)
    acc = h
    for c in range(D_FF // cw):
        ys = []
        for base in (0, D_FF):
            sl = slice(base + c * cw, base + (c + 1) * cw)
            u = _dot(hn_ext, upw_ref[:, sl])
            y = cb_ref[:, sl]
            for k in range(FFN_CONV):
                off = hr - (FFN_CONV - 1) + k
                y = y + cw_ref[k:k + 1, sl] * u[off:off + tm, :]
            ys.append(y)
        act = (_silu(ys[0]) * ys[1]).astype(BF16)
        acc = acc + _dot(act, dw_ref[c * cw:(c + 1) * cw, :])
    o_ref[...] = acc


def _conv_ffn(h, nw, up_w, conv_w, conv_b, down_w, *, seq, tm, cw):
    t, d = h.shape
    hr = BF16_ROWS
    const = lambda shape: pl.BlockSpec(shape, lambda i: (0, 0))
    return pl.pallas_call(
        functools.partial(_ffn_kernel, tm=tm, seq=seq, cw=cw),
        grid=(t // tm,),
        in_specs=[
            pl.BlockSpec((tm, d), lambda i: (i, 0)),
            pl.BlockSpec((hr, d), lambda i: (jnp.maximum(i * (tm // hr) - 1, 0), 0)),
            const((1, d)),
            const(up_w.shape),
            const(conv_w.shape),
            const(conv_b.shape),
            const(down_w.shape),
        ],
        out_specs=pl.BlockSpec((tm, d), lambda i: (i, 0)),
        out_shape=jax.ShapeDtypeStruct((t, d), F32),
        compiler_params=pltpu.CompilerParams(
            dimension_semantics=("parallel",), vmem_limit_bytes=VMEM_LIMIT),
        name="conv_ffn",
    )(h, h, nw, up_w, conv_w, conv_b, down_w)


def _mlstm_kernel(q_ref, k_ref, v_ref, og_ref, ig_ref, fg_ref, ib_ref, fb_ref, nw_ref,
                  o_ref, cn_ref, m_ref):
    c = pl.program_id(1)
    l = CHUNK
    kd = ML_QK_DIM
    vd = ML_V_DIM

    @pl.when(c == 0)
    def _():
        cn_ref[...] = jnp.zeros(cn_ref.shape, F32)
        m_ref[...] = jnp.zeros(m_ref.shape, F32)

    cap = lambda x: ML_GATE_CAP * jnp.tanh(x * (1.0 / ML_GATE_CAP))
    logi = cap(ig_ref[...] + ib_ref[...])
    fc = cap(fg_ref[...] + fb_ref[...])
    logf = jnp.minimum(fc, 0.0) - jnp.log1p(jnp.exp(-jnp.abs(fc)))
    tri = _tril_incl(l)
    bcs = _tri_cumsum(tri.astype(BF16), logf)
    bcs_t = bcs.T
    logi_t = logi.T
    glast = bcs[l - 1:l, :]
    m_all = m_ref[...]

    q = q_ref[...]
    k = k_ref[...]
    k_t = k.astype(F32).T.astype(BF16)
    for h in range(ML_HEADS):
        qh = (q[:, h * kd:(h + 1) * kd].astype(F32) * (kd ** -0.5)).astype(BF16)
        kh = k[:, h * kd:(h + 1) * kd]
        vh = v_ref[:, h * vd:(h + 1) * vd]
        bcol = bcs[:, h:h + 1]
        m_prev = m_all[:, h:h + 1]
        dm = jnp.where(tri, bcol - bcs_t[h:h + 1, :] + logi_t[h:h + 1, :], -jnp.inf)
        inter = bcol + m_prev
        m_t = jnp.maximum(inter, jnp.max(dm, axis=-1, keepdims=True))
        sw = jnp.exp(dm - m_t) * _dot_nt(qh, kh)
        iw = jnp.exp(inter - m_t)
        cn = cn_ref[h]
        qc = _dot(qh, cn.astype(BF16))
        num = _dot(sw.astype(BF16), vh) + qc[:, 0:vd] * iw
        den = jnp.sum(sw, axis=-1, keepdims=True) + qc[:, vd:vd + 1] * iw
        hm = num / jnp.maximum(jnp.abs(den), jnp.exp(-m_t))
        hm = _rms(hm, nw_ref[:, h * vd:(h + 1) * vd])
        og = og_ref[:, h * vd:(h + 1) * vd].astype(F32)
        o_ref[:, h * vd:(h + 1) * vd] = (hm * _sigmoid(og)).astype(o_ref.dtype)

        g = glast[:, h:h + 1]
        w_end = g - bcol + logi[:, h:h + 1]
        m_loc = jnp.max(w_end, axis=0, keepdims=True)
        e_end = jnp.exp(w_end - m_loc)
        xe = jnp.concatenate(
            [vh.astype(F32) * e_end, jnp.broadcast_to(e_end, (l, vd))], axis=1).astype(BF16)
        loc = _dot(k_t[h * kd:(h + 1) * kd, :], xe)
        m_new = jnp.maximum(g + m_prev, m_loc)
        cn_ref[h] = jnp.exp(g + m_prev - m_new) * cn + jnp.exp(m_loc - m_new) * loc
        m_ref[:, h:h + 1] = m_new


def _mlstm(p1, gates, i_b, f_b, norm_w, *, bsz, seq):
    nc = seq // CHUNK
    l = CHUNK
    row = lambda b, c: b * nc + c
    full = lambda shape: pl.BlockSpec(shape, lambda b, c: (0, 0))
    return pl.pallas_call(
        _mlstm_kernel,
        grid=(bsz, nc),
        in_specs=[
            pl.BlockSpec((l, ML_QK), lambda b, c: (row(b, c), 0)),
            pl.BlockSpec((l, ML_QK), lambda b, c: (row(b, c), 1)),
            pl.BlockSpec((l, ML_V), lambda b, c: (row(b, c), 1)),
            pl.BlockSpec((l, ML_V), lambda b, c: (row(b, c), 2)),
            pl.BlockSpec((l, LANES), lambda b, c: (row(b, c), 0)),
            pl.BlockSpec((l, LANES), lambda b, c: (row(b, c), 1)),
            full((1, LANES)),
            full((1, LANES)),
            full((1, ML_V)),
        ],
        out_specs=pl.BlockSpec((l, ML_V), lambda b, c: (row(b, c), 0)),
        out_shape=jax.ShapeDtypeStruct((bsz * seq, ML_V), BF16),
        scratch_shapes=[
            pltpu.VMEM((ML_HEADS, ML_QK_DIM, 2 * ML_V_DIM), F32),
            pltpu.VMEM((1, LANES), F32),
        ],
        compiler_params=pltpu.CompilerParams(
            dimension_semantics=("parallel", "arbitrary"), vmem_limit_bytes=VMEM_LIMIT),
        name="mlstm_scan",
    )(p1, p1, p1, p1, gates, gates, i_b, f_b, norm_w)


def _pad_cols(w, n):
    return jnp.pad(w, ((0, 0), (0, n - w.shape[1])))


def _row(v, n=None):
    v = v.reshape(1, -1).astype(F32)
    return v if n is None else _pad_cols(v, n)


def kernel(x, norm_w, hy_in_w, ssd_conv_w, ssd_conv_b, ssd_dt_bias, ssd_a_log, ssd_d, ssd_norm_w,
           sb_q_norm_w, sb_k_norm_w, hy_out_w, ml_in_w, ml_i_b, ml_f_b, ml_norm_w, ml_out_w,
           ffn_up_w, ffn_conv_w, ffn_conv_b, ffn_down_w):
    bsz, seq, d = x.shape
    t = bsz * seq
    tm = min(512, seq)
    h = x.reshape(t, d)

    w = hy_in_w[0]
    o_xbc = SSD_INNER
    o_dt = o_xbc + SSD_INNER + SSD_BC
    o_q = o_dt + SSD_HEADS
    w_z = w[:, 0:SSD_INNER]
    w_x = w[:, o_xbc:o_xbc + SSD_INNER]
    w_bc = w[:, o_xbc + SSD_INNER:o_dt]
    w_dt = w[:, o_dt:o_q]
    w_qkv = w[:, o_q:]
    w_main = jnp.concatenate([w_z, w_qkv, w_x, w_bc], axis=1).astype(BF16)
    w_small = _pad_cols(w_dt, LANES).astype(BF16)
    p0, dt_raw = _norm_proj(h, _row(norm_w[0, 0]), w_main, w_small, tm=tm, tn=512)

    y_ssd = _ssd(p0, dt_raw, ssd_conv_w[0], _row(ssd_conv_b[0]), _row(ssd_dt_bias[0], LANES),
                 _row(ssd_a_log[0], LANES), _row(jnp.repeat(ssd_d[0], SSD_HEAD_DIM)),
                 _row(ssd_norm_w[0]), bsz=bsz, seq=seq)
    y_sb = _stick_breaking(p0, _row(jnp.tile(sb_q_norm_w[0], 2)), _row(jnp.tile(sb_k_norm_w[0], 2)),
                           bsz=bsz, seq=seq)
    w_out = hy_out_w[0].astype(BF16)
    h = _out_proj([y_ssd, y_sb], [w_out[0:SSD_INNER], w_out[SSD_INNER:]], h, tm=tm)
    h = _conv_ffn(h, _row(norm_w[0, 1]), ffn_up_w[0].astype(BF16), ffn_conv_w[0], _row(ffn_conv_b[0]),
                  ffn_down_w[0].astype(BF16), seq=seq, tm=tm, cw=D_FF // 2)

    w = ml_in_w[0]
    o_g = 2 * ML_QK + 2 * ML_V
    w_main = w[:, 0:o_g].astype(BF16)
    w_small = jnp.concatenate(
        [_pad_cols(w[:, o_g:o_g + ML_HEADS], LANES), _pad_cols(w[:, o_g + ML_HEADS:], LANES)],
        axis=1).astype(BF16)
    p1, gates = _norm_proj(h, _row(norm_w[1, 0]), w_main, w_small, tm=tm, tn=512)
    y_ml = _mlstm(p1, gates, _row(ml_i_b[0], LANES), _row(ml_f_b[0], LANES), _row(ml_norm_w[0]),
                  bsz=bsz, seq=seq)
    h = _out_proj([y_ml], [ml_out_w[0].astype(BF16)], h, tm=tm)
    h = _conv_ffn(h, _row(norm_w[1, 1]), ffn_up_w[1].astype(BF16), ffn_conv_w[1], _row(ffn_conv_b[1]),
                  ffn_down_w[1].astype(BF16), seq=seq, tm=tm, cw=D_FF // 2)
    return h.reshape(bsz, seq, d)
```

```python
import functools

import jax
import jax.numpy as jnp
from jax import lax
from jax.experimental import pallas as pl
from jax.experimental.pallas import tpu as pltpu

F32 = jnp.float32
BF16 = jnp.bfloat16

D_MODEL = 1024
SSD_HEADS = 16
SSD_HEAD_DIM = 64
SSD_INNER = SSD_HEADS * SSD_HEAD_DIM
SSD_GROUPS = 2
SSD_STATE = 128
SSD_CONV = 4
SSD_BC = 2 * SSD_GROUPS * SSD_STATE
SB_HEADS = 16
SB_HEAD_DIM = 64
SB_INNER = SB_HEADS * SB_HEAD_DIM
ML_HEADS = 8
ML_QK_DIM = 64
ML_V_DIM = 128
ML_QK = ML_HEADS * ML_QK_DIM
ML_V = ML_HEADS * ML_V_DIM
ML_GATE_CAP = 15.0
D_FF = 2816
FFN_CONV = 3
RMS_EPS = 1e-6

CHUNK = 128
SB_TILE = 256
LOG2E = 1.4426950408889634
SB_HEADS_PER_STEP = 4
LANES = 128
BF16_ROWS = 16
HALO = 8
VMEM_LIMIT = 56 * 1024 * 1024

P0_Z, P0_Q, P0_K, P0_V, P0_X, P0_BC = 0, 1024, 2048, 3072, 4096, 5120
P0_COLS = 5632
P1_COLS = 3072


def _sigmoid(x):
    return 1.0 / (1.0 + jnp.exp(-x))


def _silu(x):
    return x * _sigmoid(x)


def _softplus(x):
    return jnp.maximum(x, 0.0) + jnp.log1p(jnp.exp(-jnp.abs(x)))


def _rms(x, w):
    ms = jnp.mean(x * x, axis=-1, keepdims=True)
    return x * lax.rsqrt(ms + RMS_EPS) * w


def _neg_abs(x):
    bits = pltpu.bitcast(x, jnp.uint32) | jnp.uint32(0x80000000)
    return pltpu.bitcast(bits, F32)


def _dot(a, b):
    return jnp.dot(a, b, preferred_element_type=F32)


def _dot_nt(a, b):
    return lax.dot_general(a, b, (((1,), (1,)), ((), ())), preferred_element_type=F32)


def _split3(v):
    hi = v.astype(BF16)
    r1 = v - hi.astype(F32)
    mid = r1.astype(BF16)
    lo = (r1 - mid.astype(F32)).astype(BF16)
    return hi, mid, lo


def _tri_cumsum(tri, v):
    hi, mid, lo = _split3(v)
    return _dot(tri, hi) + _dot(tri, mid) + _dot(tri, lo)


def _tril_incl(n):
    r = lax.broadcasted_iota(jnp.int32, (n, n), 0)
    c = lax.broadcasted_iota(jnp.int32, (n, n), 1)
    return c <= r


def _norm_proj_kernel(x_ref, nw_ref, w_ref, ws_ref, o_ref, os_ref, *, tn):
    xn = _rms(x_ref[...], nw_ref[...]).astype(BF16)
    os_ref[...] = _dot(xn, ws_ref[...])
    for j in range(w_ref.shape[1] // tn):
        cols = slice(j * tn, (j + 1) * tn)
        o_ref[:, cols] = _dot(xn, w_ref[:, cols]).astype(o_ref.dtype)


def _resident(shape):
    return pl.BlockSpec(shape, lambda *_: (0,) * len(shape), pipeline_mode=pl.Buffered(1))


def _norm_proj(x, nw, w_main, w_small, *, tm, tn):
    t, d = x.shape
    n = w_main.shape[1]
    ns = w_small.shape[1]
    return pl.pallas_call(
        functools.partial(_norm_proj_kernel, tn=tn),
        grid=(t // tm,),
        in_specs=[
            pl.BlockSpec((tm, d), lambda i: (i, 0)),
            _resident((1, d)),
            _resident((d, n)),
            _resident((d, ns)),
        ],
        out_specs=[
            pl.BlockSpec((tm, n), lambda i: (i, 0)),
            pl.BlockSpec((tm, ns), lambda i: (i, 0)),
        ],
        out_shape=[jax.ShapeDtypeStruct((t, n), BF16), jax.ShapeDtypeStruct((t, ns), F32)],
        compiler_params=pltpu.CompilerParams(
            dimension_semantics=("parallel",), vmem_limit_bytes=VMEM_LIMIT),
        name="norm_proj",
    )(x, nw, w_main, w_small)


def _ssd_kernel(z_ref, x_ref, bc_ref, dt_ref, cw_ref, cb_ref, dtb_ref, alog_ref, dsk_ref, nw_ref,
                o_ref, cbuf_ref, state_ref):
    c = pl.program_id(1)
    l = CHUNK
    p = SSD_HEAD_DIM
    hpg = SSD_HEADS // SSD_GROUPS

    @pl.when(c == 0)
    def _():
        cbuf_ref[0:HALO, :] = jnp.zeros((HALO, SSD_INNER + SSD_BC), F32)
        state_ref[...] = jnp.zeros(state_ref.shape, F32)

    cbuf_ref[HALO:HALO + l, 0:SSD_INNER] = x_ref[...].astype(F32)
    cbuf_ref[HALO:HALO + l, SSD_INNER:] = bc_ref[...].astype(F32)
    conv = cb_ref[...]
    for k in range(SSD_CONV):
        off = HALO - (SSD_CONV - 1) + k
        conv = conv + cw_ref[k:k + 1, :] * cbuf_ref[off:off + l, :]
    cbuf_ref[0:HALO, :] = cbuf_ref[l:l + HALO, :]
    xbc = _silu(conv)
    xs = xbc[:, 0:SSD_INNER]

    dt = _softplus(dt_ref[...] + dtb_ref[...])
    da = dt * (-jnp.exp(alog_ref[...]))
    tri = _tril_incl(l)
    acs = _tri_cumsum(tri.astype(BF16), da)
    acs_t = acs.T
    last = acs[l - 1:l, :]
    dend = jnp.exp(last - acs)
    cdec = jnp.exp(last)
    eacs = jnp.exp(acs)

    ys = []
    for g in range(SSD_GROUPS):
        bm = xbc[:, SSD_INNER + g * SSD_STATE:SSD_INNER + (g + 1) * SSD_STATE]
        cm = xbc[:, SSD_INNER + (SSD_GROUPS + g) * SSD_STATE:SSD_INNER + (SSD_GROUPS + g + 1) * SSD_STATE]
        bm_b = bm.astype(BF16)
        cm_b = cm.astype(BF16)
        cb = _dot_nt(cm_b, bm_b)
        bm_t = bm.T.astype(BF16)
        for r in range(hpg):
            h = g * hpg + r
            seg = acs[:, h:h + 1] - acs_t[h:h + 1, :]
            decay = jnp.exp(jnp.where(tri, seg, -jnp.inf))
            m = (cb * decay).astype(BF16)
            xd = xs[:, h * p:(h + 1) * p] * dt[:, h:h + 1]
            st = state_ref[h]
            lhs = jnp.concatenate([m, (cm * eacs[:, h:h + 1]).astype(BF16)], axis=1)
            rhs = jnp.concatenate([xd.astype(BF16), st.astype(BF16)], axis=0)
            ys.append(_dot(lhs, rhs))
            new = _dot(bm_t, (xd * dend[:, h:h + 1]).astype(BF16))
            state_ref[h] = cdec[:, h:h + 1] * st + new

    y = jnp.concatenate(ys, axis=1) + xs * dsk_ref[...]
    y = y * _silu(z_ref[...].astype(F32))
    o_ref[...] = _rms(y, nw_ref[...]).astype(o_ref.dtype)


def _ssd(p0, dt_raw, conv_w, conv_b, dt_bias, a_log, d_skip, norm_w, *, bsz, seq):
    nc = seq // CHUNK
    l = CHUNK
    row = lambda b, c: b * nc + c
    full = lambda shape: pl.BlockSpec(shape, lambda b, c: (0, 0))
    return pl.pallas_call(
        _ssd_kernel,
        grid=(bsz, nc),
        in_specs=[
            pl.BlockSpec((l, SSD_INNER), lambda b, c: (row(b, c), P0_Z // SSD_INNER)),
            pl.BlockSpec((l, SSD_INNER), lambda b, c: (row(b, c), P0_X // SSD_INNER)),
            pl.BlockSpec((l, SSD_BC), lambda b, c: (row(b, c), P0_BC // SSD_BC)),
            pl.BlockSpec((l, LANES), lambda b, c: (row(b, c), 0)),
            full((SSD_CONV, SSD_INNER + SSD_BC)),
            full((1, SSD_INNER + SSD_BC)),
            full((1, LANES)),
            full((1, LANES)),
            full((1, SSD_INNER)),
            full((1, SSD_INNER)),
        ],
        out_specs=pl.BlockSpec((l, SSD_INNER), lambda b, c: (row(b, c), 0)),
        out_shape=jax.ShapeDtypeStruct((bsz * seq, SSD_INNER), BF16),
        scratch_shapes=[
            pltpu.VMEM((l + HALO, SSD_INNER + SSD_BC), F32),
            pltpu.VMEM((SSD_HEADS, SSD_STATE, SSD_HEAD_DIM), F32),
        ],
        compiler_params=pltpu.CompilerParams(
            dimension_semantics=("parallel", "arbitrary"), vmem_limit_bytes=VMEM_LIMIT),
        name="ssd_scan",
    )(p0, p0, p0, dt_raw, conv_w, conv_b, dt_bias, a_log, d_skip, norm_w)


def _head_rms(x, w):
    dh = SB_HEAD_DIM
    lane = lax.broadcasted_iota(jnp.int32, (x.shape[0], LANES), 1)
    lo = lane < dh
    outs = []
    for s in range(x.shape[1] // LANES):
        xs = x[:, s * LANES:(s + 1) * LANES]
        x2 = xs * xs
        s0 = jnp.sum(jnp.where(lo, x2, 0.0), axis=-1, keepdims=True)
        s1 = jnp.sum(jnp.where(lo, 0.0, x2), axis=-1, keepdims=True)
        ms = jnp.where(lo, s0, s1) * (1.0 / dh)
        outs.append(xs * lax.rsqrt(ms + RMS_EPS))
    return jnp.concatenate(outs, axis=1) * w


def _sb_kernel(q_ref, k_ref, v_ref, qw_ref, kw_ref, w2_ref, o_ref, kt_ref, vz_ref, carry_ref, acc_ref,
               lw_ref, tot_ref, *, nkb):
    qi = pl.program_id(2)
    tq = SB_TILE
    sub = CHUNK
    dh = SB_HEAD_DIM
    nh = SB_HEADS_PER_STEP
    width = nh * dh

    @pl.when(qi == 0)
    def _():
        head_of_lane = lax.broadcasted_iota(jnp.int32, (tq, width), 1) // dh
        for kb in range(nkb):
            rows = slice(kb * tq, (kb + 1) * tq)
            kn = _head_rms(k_ref[rows, :].astype(F32), kw_ref[...])
            kt_ref[kb] = kn.T.astype(BF16)
            v = v_ref[rows, :].astype(F32)
            for hh in range(nh):
                vz_ref[kb, hh * tq:(hh + 1) * tq, :] = jnp.where(head_of_lane == hh, v, 0.0).astype(BF16)

    qn = (_head_rms(q_ref[...].astype(F32), qw_ref[...]) * (dh ** -0.5 * LOG2E)).astype(BF16)
    w2 = w2_ref[...]
    row = lax.broadcasted_iota(jnp.int32, (tq, tq), 0)
    col = lax.broadcasted_iota(jnp.int32, (tq, tq), 1)
    causal = col < row

    def logits(kb):
        kt = kt_ref[kb]
        return [_dot(qn[:, hh * dh:(hh + 1) * dh], kt[hh * dh:(hh + 1) * dh, :]) for hh in range(nh)]

    def stage_logits(zs, diag):
        rs = []
        for hh in range(nh):
            z = zs[hh]
            sp = jnp.maximum(z, 0.0) + jnp.log2(1.0 + jnp.exp2(_neg_abs(z)))
            spm = jnp.where(causal, sp, 0.0) if diag else sp
            hi = spm.astype(BF16)
            lo = (spm - hi.astype(F32)).astype(BF16)
            r1 = _dot(jnp.concatenate([hi[:, sub:], lo[:, sub:]], axis=1), w2)
            r0 = _dot(jnp.concatenate([hi[:, :sub], lo[:, :sub]], axis=1), w2)
            rs.append((z - sp, r1, r0))
        for hh in range(nh):
            base, r1, r0 = rs[hh]
            t1 = r1[:, sub:]
            lw = jnp.concatenate([base[:, :sub] + (r0[:, :sub] + t1), base[:, sub:] + r1[:, :sub]], axis=1)
            lw_ref[hh] = jnp.where(causal, lw, -jnp.inf) if diag else lw
            tot_ref[hh] = t1 + r0[:, sub:]

    def weights():
        ws = []
        for hh in range(nh):
            c = carry_ref[hh]
            lw = lw_ref[hh]
            w = jnp.exp2(jnp.concatenate([lw[:, :sub] + c, lw[:, sub:] + c], axis=1))
            carry_ref[hh] = c + tot_ref[hh]
            ws.append(w.astype(BF16))
        return jnp.concatenate(ws, axis=1)

    carry_ref[...] = jnp.zeros(carry_ref.shape, F32)
    acc_ref[...] = jnp.zeros(acc_ref.shape, F32)
    stage_logits(logits(qi), True)

    def body(i, c):
        zs = logits(qi - 1 - i)
        pv = _dot(weights(), vz_ref[qi - i])
        stage_logits(zs, False)
        acc_ref[...] += pv
        return c

    lax.fori_loop(0, qi, body, 0)
    acc_ref[...] += _dot(weights(), vz_ref[0])
    o_ref[...] = acc_ref[...].astype(o_ref.dtype)


def _stick_breaking(p0, q_norm_w, k_norm_w, *, bsz, seq):
    tq = SB_TILE
    sub = CHUNK
    nh = SB_HEADS_PER_STEP
    width = nh * SB_HEAD_DIM
    nq = seq // tq
    r = jnp.arange(2 * sub)[:, None] % sub
    cidx = jnp.arange(2 * sub)[None, :]
    w2 = jnp.where((cidx >= sub) | (r > cidx), -1.0, 0.0).astype(BF16)
    return pl.pallas_call(
        functools.partial(_sb_kernel, nkb=nq),
        grid=(bsz, SB_HEADS // nh, nq),
        in_specs=[
            pl.BlockSpec((tq, width), lambda b, h, i: (b * nq + i, P0_Q // width + h)),
            pl.BlockSpec((seq, width), lambda b, h, i: (b, P0_K // width + h)),
            pl.BlockSpec((seq, width), lambda b, h, i: (b, P0_V // width + h)),
            pl.BlockSpec((1, width), lambda b, h, i: (0, 0)),
            pl.BlockSpec((1, width), lambda b, h, i: (0, 0)),
            pl.BlockSpec((2 * sub, 2 * sub), lambda b, h, i: (0, 0)),
        ],
        out_specs=pl.BlockSpec((tq, width), lambda b, h, i: (b * nq + i, h)),
        out_shape=jax.ShapeDtypeStruct((bsz * seq, SB_INNER), BF16),
        scratch_shapes=[
            pltpu.VMEM((nq, width, tq), BF16),
            pltpu.VMEM((nq, nh * tq, width), BF16),
            pltpu.VMEM((nh, tq, sub), F32),
            pltpu.VMEM((tq, width), F32),
            pltpu.VMEM((nh, tq, tq), F32),
            pltpu.VMEM((nh, tq, sub), F32),
        ],
        compiler_params=pltpu.CompilerParams(
            dimension_semantics=("parallel", "parallel", "arbitrary"), vmem_limit_bytes=VMEM_LIMIT),
        name="stick_breaking",
    )(p0, p0, p0, q_norm_w, k_norm_w, w2)


def _out_proj_kernel(*refs, n_in):
    a_refs = refs[:n_in]
    w_refs = refs[n_in:2 * n_in]
    x_ref = refs[2 * n_in]
    o_ref = refs[2 * n_in + 1]
    acc = x_ref[...]
    for a_ref, w_ref in zip(a_refs, w_refs):
        acc = acc + _dot(a_ref[...], w_ref[...])
    o_ref[...] = acc


def _out_proj(acts, weights, x, *, tm):
    t, d = x.shape
    n_in = len(acts)
    in_specs = [pl.BlockSpec((tm, a.shape[1]), lambda i: (i, 0)) for a in acts]
    in_specs += [_resident(w.shape) for w in weights]
    in_specs += [pl.BlockSpec((tm, d), lambda i: (i, 0))]
    return pl.pallas_call(
        functools.partial(_out_proj_kernel, n_in=n_in),
        grid=(t // tm,),
        in_specs=in_specs,
        out_specs=pl.BlockSpec((tm, d), lambda i: (i, 0)),
        out_shape=jax.ShapeDtypeStruct((t, d), F32),
        compiler_params=pltpu.CompilerParams(
            dimension_semantics=("parallel",), vmem_limit_bytes=VMEM_LIMIT),
        name="out_proj",
    )(*acts, *weights, x)


def _ffn_kernel(h_ref, halo_ref, nw_ref, upw_ref, cw_ref, cb_ref, dw_ref, o_ref, *, tm, seq, cw):
    i = pl.program_id(0)
    hr = BF16_ROWS
    h = h_ref[...]
    keep = jnp.where(lax.rem(i * tm, seq) == 0, 0.0, 1.0)
    hn_ext = jnp.concatenate(
        [_rms(halo_ref[...], nw_ref[...]) * keep, _rms(h, nw_ref[...])], axis=0).astype(BF16)
    acc = h
    for c in range(D_FF // cw):
        ys = []
        for base in (0, D_FF):
            sl = slice(base + c * cw, base + (c + 1) * cw)
            u = _dot(hn_ext, upw_ref[:, sl])
            y = cb_ref[:, sl]
            for k in range(FFN_CONV):
                off = hr - (FFN_CONV - 1) + k
                y = y + cw_ref[k:k + 1, sl] * u[off:off + tm, :]
            ys.append(y)
        act = (_silu(ys[0]) * ys[1]).astype(BF16)
        acc = acc + _dot(act, dw_ref[c * cw:(c + 1) * cw, :])
    o_ref[...] = acc


def _conv_ffn(h, nw, up_w, conv_w, conv_b, down_w, *, seq, tm, cw):
    t, d = h.shape
    hr = BF16_ROWS
    const = _resident
    return pl.pallas_call(
        functools.partial(_ffn_kernel, tm=tm, seq=seq, cw=cw),
        grid=(t // tm,),
        in_specs=[
            pl.BlockSpec((tm, d), lambda i: (i, 0)),
            pl.BlockSpec((hr, d), lambda i: (jnp.maximum(i * (tm // hr) - 1, 0), 0)),
            const((1, d)),
            const(up_w.shape),
            const(conv_w.shape),
            const(conv_b.shape),
            const(down_w.shape),
        ],
        out_specs=pl.BlockSpec((tm, d), lambda i: (i, 0)),
        out_shape=jax.ShapeDtypeStruct((t, d), F32),
        compiler_params=pltpu.CompilerParams(
            dimension_semantics=("parallel",), vmem_limit_bytes=VMEM_LIMIT),
        name="conv_ffn",
    )(h, h, nw, up_w, conv_w, conv_b, down_w)


def _mlstm_kernel(q_ref, k_ref, v_ref, og_ref, ig_ref, fg_ref, ib_ref, fb_ref, nw_ref,
                  o_ref, cn_ref, m_ref):
    c = pl.program_id(1)
    l = CHUNK
    kd = ML_QK_DIM
    vd = ML_V_DIM

    @pl.when(c == 0)
    def _():
        cn_ref[...] = jnp.zeros(cn_ref.shape, F32)
        m_ref[...] = jnp.zeros(m_ref.shape, F32)

    cap = lambda x: ML_GATE_CAP * jnp.tanh(x * (1.0 / ML_GATE_CAP))
    logi = cap(ig_ref[...] + ib_ref[...])
    fc = cap(fg_ref[...] + fb_ref[...])
    logf = jnp.minimum(fc, 0.0) - jnp.log1p(jnp.exp(-jnp.abs(fc)))
    tri = _tril_incl(l)
    bcs = _tri_cumsum(tri.astype(BF16), logf)
    bcs_t = bcs.T
    logi_t = logi.T
    glast = bcs[l - 1:l, :]
    m_all = m_ref[...]

    q = q_ref[...]
    k = k_ref[...]
    k_t = k.astype(F32).T.astype(BF16)
    heads = range(ML_HEADS)
    qs = [(q[:, h * kd:(h + 1) * kd].astype(F32) * (kd ** -0.5)).astype(BF16) for h in heads]
    vs = [v_ref[:, h * vd:(h + 1) * vd] for h in heads]
    cns = [cn_ref[h] for h in heads]
    qk = [_dot_nt(qs[h], k[:, h * kd:(h + 1) * kd]) for h in heads]
    qc = [_dot(qs[h], cns[h].astype(BF16)) for h in heads]
    bcol = [bcs[:, h:h + 1] for h in heads]
    m_prev = [m_all[:, h:h + 1] for h in heads]
    dm = [jnp.where(tri, bcol[h] - bcs_t[h:h + 1, :] + logi_t[h:h + 1, :], -jnp.inf) for h in heads]
    inter = [bcol[h] + m_prev[h] for h in heads]
    m_t = [jnp.maximum(inter[h], jnp.max(dm[h], axis=-1, keepdims=True)) for h in heads]
    sw = [jnp.exp(dm[h] - m_t[h]) * qk[h] for h in heads]
    iw = [jnp.exp(inter[h] - m_t[h]) for h in heads]
    num = [_dot(sw[h].astype(BF16), vs[h]) + qc[h][:, 0:vd] * iw[h] for h in heads]
    den = [jnp.sum(sw[h], axis=-1, keepdims=True) + qc[h][:, vd:vd + 1] * iw[h] for h in heads]
    hm = [num[h] / jnp.maximum(jnp.abs(den[h]), jnp.exp(-m_t[h])) for h in heads]
    hm = [_rms(hm[h], nw_ref[:, h * vd:(h + 1) * vd]) for h in heads]
    for h in heads:
        og = og_ref[:, h * vd:(h + 1) * vd].astype(F32)
        o_ref[:, h * vd:(h + 1) * vd] = (hm[h] * _sigmoid(og)).astype(o_ref.dtype)

    g = [glast[:, h:h + 1] for h in heads]
    w_end = [g[h] - bcol[h] + logi[:, h:h + 1] for h in heads]
    m_loc = [jnp.max(w_end[h], axis=0, keepdims=True) for h in heads]
    e_end = [jnp.exp(w_end[h] - m_loc[h]) for h in heads]
    xe = [jnp.concatenate([vs[h].astype(F32) * e_end[h], jnp.broadcast_to(e_end[h], (l, vd))],
                          axis=1).astype(BF16) for h in heads]
    loc = [_dot(k_t[h * kd:(h + 1) * kd, :], xe[h]) for h in heads]
    for h in heads:
        m_new = jnp.maximum(g[h] + m_prev[h], m_loc[h])
        cn_ref[h] = jnp.exp(g[h] + m_prev[h] - m_new) * cns[h] + jnp.exp(m_loc[h] - m_new) * loc[h]
        m_ref[:, h:h + 1] = m_new


def _mlstm(p1, gates, i_b, f_b, norm_w, *, bsz, seq):
    nc = seq // CHUNK
    l = CHUNK
    row = lambda b, c: b * nc + c
    full = lambda shape: pl.BlockSpec(shape, lambda b, c: (0, 0))
    return pl.pallas_call(
        _mlstm_kernel,
        grid=(bsz, nc),
        in_specs=[
            pl.BlockSpec((l, ML_QK), lambda b, c: (row(b, c), 0)),
            pl.BlockSpec((l, ML_QK), lambda b, c: (row(b, c), 1)),
            pl.BlockSpec((l, ML_V), lambda b, c: (row(b, c), 1)),
            pl.BlockSpec((l, ML_V), lambda b, c: (row(b, c), 2)),
            pl.BlockSpec((l, LANES), lambda b, c: (row(b, c), 0)),
            pl.BlockSpec((l, LANES), lambda b, c: (row(b, c), 1)),
            full((1, LANES)),
            full((1, LANES)),
            full((1, ML_V)),
        ],
        out_specs=pl.BlockSpec((l, ML_V), lambda b, c: (row(b, c), 0)),
        out_shape=jax.ShapeDtypeStruct((bsz * seq, ML_V), BF16),
        scratch_shapes=[
            pltpu.VMEM((ML_HEADS, ML_QK_DIM, 2 * ML_V_DIM), F32),
            pltpu.VMEM((1, LANES), F32),
        ],
        compiler_params=pltpu.CompilerParams(
            dimension_semantics=("parallel", "arbitrary"), vmem_limit_bytes=VMEM_LIMIT),
        name="mlstm_scan",
    )(p1, p1, p1, p1, gates, gates, i_b, f_b, norm_w)


def _pad_cols(w, n):
    return jnp.pad(w, ((0, 0), (0, n - w.shape[1])))


def _row(v, n=None):
    v = v.reshape(1, -1).astype(F32)
    return v if n is None else _pad_cols(v, n)


def kernel(x, norm_w, hy_in_w, ssd_conv_w, ssd_conv_b, ssd_dt_bias, ssd_a_log, ssd_d, ssd_norm_w,
           sb_q_norm_w, sb_k_norm_w, hy_out_w, ml_in_w, ml_i_b, ml_f_b, ml_norm_w, ml_out_w,
           ffn_up_w, ffn_conv_w, ffn_conv_b, ffn_down_w):
    bsz, seq, d = x.shape
    t = bsz * seq
    tm = min(512, seq)
    h = x.reshape(t, d)

    w = hy_in_w[0]
    o_xbc = SSD_INNER
    o_dt = o_xbc + SSD_INNER + SSD_BC
    o_q = o_dt + SSD_HEADS
    w_z = w[:, 0:SSD_INNER]
    w_x = w[:, o_xbc:o_xbc + SSD_INNER]
    w_bc = w[:, o_xbc + SSD_INNER:o_dt]
    w_dt = w[:, o_dt:o_q]
    w_qkv = w[:, o_q:]
    w_main = jnp.concatenate([w_z, w_qkv, w_x, w_bc], axis=1).astype(BF16)
    w_small = _pad_cols(w_dt, LANES).astype(BF16)
    p0, dt_raw = _norm_proj(h, _row(norm_w[0, 0]), w_main, w_small, tm=tm, tn=512)

    y_ssd = _ssd(p0, dt_raw, ssd_conv_w[0], _row(ssd_conv_b[0]), _row(ssd_dt_bias[0], LANES),
                 _row(ssd_a_log[0], LANES), _row(jnp.repeat(ssd_d[0], SSD_HEAD_DIM)),
                 _row(ssd_norm_w[0]), bsz=bsz, seq=seq)
    y_sb = _stick_breaking(p0, _row(jnp.tile(sb_q_norm_w[0], SB_HEADS_PER_STEP)),
                           _row(jnp.tile(sb_k_norm_w[0], SB_HEADS_PER_STEP)),
                           bsz=bsz, seq=seq)
    w_out = hy_out_w[0].astype(BF16)
    h = _out_proj([y_ssd, y_sb], [w_out[0:SSD_INNER], w_out[SSD_INNER:]], h, tm=tm)
    h = _conv_ffn(h, _row(norm_w[0, 1]), ffn_up_w[0].astype(BF16), ffn_conv_w[0], _row(ffn_conv_b[0]),
                  ffn_down_w[0].astype(BF16), seq=seq, tm=tm, cw=D_FF // 2)

    w = ml_in_w[0]
    o_g = 2 * ML_QK + 2 * ML_V
    w_main = w[:, 0:o_g].astype(BF16)
    w_small = jnp.concatenate(
        [_pad_cols(w[:, o_g:o_g + ML_HEADS], LANES), _pad_cols(w[:, o_g + ML_HEADS:], LANES)],
        axis=1).astype(BF16)
    p1, gates = _norm_proj(h, _row(norm_w[1, 0]), w_main, w_small, tm=tm, tn=512)
    y_ml = _mlstm(p1, gates, _row(ml_i_b[0], LANES), _row(ml_f_b[0], LANES), _row(ml_norm_w[0]),
                  bsz=bsz, seq=seq)
    h = _out_proj([y_ml], [ml_out_w[0].astype(BF16)], h, tm=tm)
    h = _conv_ffn(h, _row(norm_w[1, 1]), ffn_up_w[1].astype(BF16), ffn_conv_w[1], _row(ffn_conv_b[1]),
                  ffn_down_w[1].astype(BF16), seq=seq, tm=tm, cw=D_FF // 2)
    return h.reshape(bsz, seq, d)
```

```python
import functools

import jax
import jax.numpy as jnp
from jax import lax
from jax.experimental import pallas as pl
from jax.experimental.pallas import tpu as pltpu

F32 = jnp.float32
BF16 = jnp.bfloat16

D_MODEL = 1024
SSD_HEADS = 16
SSD_HEAD_DIM = 64
SSD_INNER = SSD_HEADS * SSD_HEAD_DIM
SSD_GROUPS = 2
SSD_STATE = 128
SSD_CONV = 4
SSD_BC = 2 * SSD_GROUPS * SSD_STATE
SB_HEADS = 16
SB_HEAD_DIM = 64
SB_INNER = SB_HEADS * SB_HEAD_DIM
ML_HEADS = 8
ML_QK_DIM = 64
ML_V_DIM = 128
ML_QK = ML_HEADS * ML_QK_DIM
ML_V = ML_HEADS * ML_V_DIM
ML_GATE_CAP = 15.0
D_FF = 2816
FFN_CONV = 3
RMS_EPS = 1e-6

CHUNK = 128
SB_TILE = 256
LOG2E = 1.4426950408889634
SB_HEADS_PER_STEP = 8
SB_PV_HEADS = 4
LANES = 128
BF16_ROWS = 16
VMEM_LIMIT = 56 * 1024 * 1024

P0_Z, P0_Q, P0_K, P0_V, P0_X, P0_BC = 0, 1024, 2048, 3072, 4096, 5120
P0_COLS = 5632
P1_COLS = 3072


def _sigmoid(x):
    return 1.0 / (1.0 + jnp.exp(-x))


def _silu(x):
    return x * _sigmoid(x)


def _softplus(x):
    return jnp.maximum(x, 0.0) + jnp.log1p(jnp.exp(-jnp.abs(x)))


def _rms(x, w):
    ms = jnp.mean(x * x, axis=-1, keepdims=True)
    return x * lax.rsqrt(ms + RMS_EPS) * w


def _neg_abs(x):
    bits = pltpu.bitcast(x, jnp.uint32) | jnp.uint32(0x80000000)
    return pltpu.bitcast(bits, F32)


def _dot(a, b):
    return jnp.dot(a, b, preferred_element_type=F32)


def _dot_nt(a, b):
    return lax.dot_general(a, b, (((1,), (1,)), ((), ())), preferred_element_type=F32)


def _split3(v):
    hi = v.astype(BF16)
    r1 = v - hi.astype(F32)
    mid = r1.astype(BF16)
    lo = (r1 - mid.astype(F32)).astype(BF16)
    return hi, mid, lo


def _tri_cumsum(tri, v):
    hi, mid, lo = _split3(v)
    return _dot(tri, hi) + _dot(tri, mid) + _dot(tri, lo)


def _tril_incl(n):
    r = lax.broadcasted_iota(jnp.int32, (n, n), 0)
    c = lax.broadcasted_iota(jnp.int32, (n, n), 1)
    return c <= r


def _norm_proj_kernel(x_ref, nw_ref, w_ref, ws_ref, o_ref, os_ref, *, tn):
    xn = _rms(x_ref[...], nw_ref[...]).astype(BF16)
    os_ref[...] = _dot(xn, ws_ref[...])
    for j in range(w_ref.shape[1] // tn):
        cols = slice(j * tn, (j + 1) * tn)
        o_ref[:, cols] = _dot(xn, w_ref[:, cols]).astype(o_ref.dtype)


def _resident(shape):
    return pl.BlockSpec(shape, lambda *_: (0,) * len(shape), pipeline_mode=pl.Buffered(1))


def _norm_proj(x, nw, w_main, w_small, *, tm, tn):
    t, d = x.shape
    n = w_main.shape[1]
    ns = w_small.shape[1]
    return pl.pallas_call(
        functools.partial(_norm_proj_kernel, tn=tn),
        grid=(t // tm,),
        in_specs=[
            pl.BlockSpec((tm, d), lambda i: (i, 0)),
            _resident((1, d)),
            _resident((d, n)),
            _resident((d, ns)),
        ],
        out_specs=[
            pl.BlockSpec((tm, n), lambda i: (i, 0)),
            pl.BlockSpec((tm, ns), lambda i: (i, 0)),
        ],
        out_shape=[jax.ShapeDtypeStruct((t, n), BF16), jax.ShapeDtypeStruct((t, ns), F32)],
        compiler_params=pltpu.CompilerParams(
            dimension_semantics=("parallel",), vmem_limit_bytes=VMEM_LIMIT),
        name="norm_proj",
    )(x, nw, w_main, w_small)


def _ssd_kernel(z_ref, x_ref, bc_ref, xp_ref, bcp_ref, dt_ref, cw_ref, cb_ref, dtb_ref, alog_ref, dsk_ref,
                nw_ref, ex_ref, sh_ref, o_ref, state_ref):
    c = pl.program_id(1)
    l = CHUNK
    p = SSD_HEAD_DIM
    hpg = SSD_HEADS // SSD_GROUPS

    @pl.when(c == 0)
    def _():
        state_ref[...] = jnp.zeros(state_ref.shape, F32)

    cur = jnp.concatenate([x_ref[...], bc_ref[...]], axis=1)
    prev = jnp.concatenate([xp_ref[...], bcp_ref[...]], axis=1)
    prev = jnp.where(c > 0, prev, jnp.zeros_like(prev))
    xin = jnp.concatenate([prev, cur], axis=0)
    conv = cb_ref[...] + cw_ref[SSD_CONV - 1:SSD_CONV, :] * cur.astype(F32)
    for k in range(SSD_CONV - 1):
        conv = conv + cw_ref[k:k + 1, :] * _dot(sh_ref[k], xin)
    xbc = _silu(conv)
    xs = xbc[:, 0:SSD_INNER]

    dt = _softplus(dt_ref[...] + dtb_ref[...])
    da = dt * (-jnp.exp(alog_ref[...]))
    tri = _tril_incl(l)
    acs = _tri_cumsum(tri.astype(BF16), da)
    acs_t = acs.T
    last = acs[l - 1:l, :]
    eacs = jnp.exp(acs)
    dtd = dt * jnp.exp(last - acs)

    def expand(v):
        hi = v.astype(BF16)
        lo = (v - hi.astype(F32)).astype(BF16)
        return _dot(jnp.concatenate([hi, lo], axis=1), ex_ref[...])

    dt_x = expand(dt)
    dtd_x = expand(dtd)
    eacs_x = expand(eacs)
    cdec_x = eacs_x[l - 1:l, :]
    xd = xs * dt_x
    xdd = (xs * dtd_x).astype(BF16)

    lane = lax.broadcasted_iota(jnp.int32, (l, LANES), 1)
    first = lane < p
    gw = hpg * p
    y_parts = []
    for g in range(SSD_GROUPS):
        bm = xbc[:, SSD_INNER + g * SSD_STATE:SSD_INNER + (g + 1) * SSD_STATE]
        cm = xbc[:, SSD_INNER + (SSD_GROUPS + g) * SSD_STATE:SSD_INNER + (SSD_GROUPS + g + 1) * SSD_STATE]
        cm_b = cm.astype(BF16)
        cb = _dot_nt(cm_b, bm.astype(BF16))
        cols = slice(g * gw, (g + 1) * gw)
        st = state_ref[g]
        y_off = _dot(cm_b, st.astype(BF16)) * eacs_x[:, cols]
        state_ref[g] = cdec_x[:, cols] * st + _dot(bm.T.astype(BF16), xdd[:, cols])
        for j in range(hpg // 2):
            h0 = g * hpg + 2 * j
            ms = []
            for h in (h0, h0 + 1):
                seg = acs[:, h:h + 1] - acs_t[h:h + 1, :]
                ms.append((cb * jnp.exp(jnp.where(tri, seg, -jnp.inf))).astype(BF16))
            xp = xd[:, h0 * p:(h0 + 2) * p]
            rhs = jnp.concatenate([jnp.where(first, xp, 0.0), jnp.where(first, 0.0, xp)], axis=0)
            y_parts.append(_dot(jnp.concatenate(ms, axis=1), rhs.astype(BF16))
                           + y_off[:, 2 * j * p:(2 * j + 2) * p])

    y = jnp.concatenate(y_parts, axis=1) + xs * dsk_ref[...]
    y = y * _silu(z_ref[...].astype(F32))
    o_ref[...] = _rms(y, nw_ref[...]).astype(o_ref.dtype)


def _ssd(p0, dt_raw, conv_w, conv_b, dt_bias, a_log, d_skip, norm_w, *, bsz, seq):
    nc = seq // CHUNK
    l = CHUNK
    row = lambda b, c: b * nc + c
    prev_row = lambda b, c: jnp.maximum(b * nc + c - 1, 0)
    full = lambda shape: pl.BlockSpec(shape, lambda b, c: (0, 0))
    head_of_col =jnp.arange(SSD_INNER)[None, :] // SSD_HEAD_DIM
    expand = (jnp.arange(2 * LANES)[:, None] % LANES == head_of_col).astype(BF16)
    delay = (SSD_CONV - 1) - jnp.arange(SSD_CONV - 1)[:, None, None]
    shifts = (jnp.arange(2 * l)[None, None, :] == l + jnp.arange(l)[None, :, None] - delay).astype(BF16)
    return pl.pallas_call(
        _ssd_kernel,
        grid=(bsz, nc),
        in_specs=[
            pl.BlockSpec((l, SSD_INNER), lambda b, c: (row(b, c), P0_Z // SSD_INNER)),
            pl.BlockSpec((l, SSD_INNER), lambda b, c: (row(b, c), P0_X // SSD_INNER)),
            pl.BlockSpec((l, SSD_BC), lambda b, c: (row(b, c), P0_BC // SSD_BC)),
            pl.BlockSpec((l, SSD_INNER), lambda b, c: (prev_row(b, c), P0_X // SSD_INNER)),
            pl.BlockSpec((l, SSD_BC), lambda b, c: (prev_row(b, c), P0_BC // SSD_BC)),
            pl.BlockSpec((l, LANES), lambda b, c: (row(b, c), 0)),
            full((SSD_CONV, SSD_INNER + SSD_BC)),
            full((1, SSD_INNER + SSD_BC)),
            full((1, LANES)),
            full((1, LANES)),
            full((1, SSD_INNER)),
            full((1, SSD_INNER)),
            full((2 * LANES, SSD_INNER)),
            pl.BlockSpec((SSD_CONV - 1, l, 2 * l), lambda b, c: (0, 0, 0)),
        ],
        out_specs=pl.BlockSpec((l, SSD_INNER), lambda b, c: (row(b, c), 0)),
        out_shape=jax.ShapeDtypeStruct((bsz * seq, SSD_INNER), BF16),
        scratch_shapes=[
            pltpu.VMEM((SSD_GROUPS, SSD_STATE, SSD_INNER // SSD_GROUPS), F32),
        ],
        compiler_params=pltpu.CompilerParams(
            dimension_semantics=("parallel", "arbitrary"), vmem_limit_bytes=VMEM_LIMIT),
        name="ssd_scan",
    )(p0, p0, p0, p0, p0, dt_raw, conv_w, conv_b, dt_bias, a_log, d_skip, norm_w, expand, shifts)


def _head_rms(x, w):
    dh = SB_HEAD_DIM
    lane = lax.broadcasted_iota(jnp.int32, (x.shape[0], LANES), 1)
    lo = lane < dh
    outs = []
    for s in range(x.shape[1] // LANES):
        xs = x[:, s * LANES:(s + 1) * LANES]
        x2 = xs * xs
        s0 = jnp.sum(jnp.where(lo, x2, 0.0), axis=-1, keepdims=True)
        s1 = jnp.sum(jnp.where(lo, 0.0, x2), axis=-1, keepdims=True)
        ms = jnp.where(lo, s0, s1) * (1.0 / dh)
        outs.append(xs * lax.rsqrt(ms + RMS_EPS))
    return jnp.concatenate(outs, axis=1) * w


def _sb_kernel(q_ref, k_ref, v_ref, qw_ref, kw_ref, w2_ref, o_ref, kt_ref, vz_ref, carry_ref, acc_ref,
               lw_ref, tot_ref, *, nkb):
    qi = pl.program_id(2)
    tq = SB_TILE
    sub = CHUNK
    dh = SB_HEAD_DIM
    nh = SB_HEADS_PER_STEP
    width = nh * dh
    gh = SB_PV_HEADS
    gwid = gh * dh

    @pl.when(qi == 0)
    def _():
        head_of_lane = lax.broadcasted_iota(jnp.int32, (tq, gwid), 1) // dh
        for kb in range(nkb):
            rows = slice(kb * tq, (kb + 1) * tq)
            kn = _head_rms(k_ref[rows, :].astype(F32), kw_ref[...])
            kt_ref[kb] = kn.T.astype(BF16)
            for gi in range(nh // gh):
                v = v_ref[rows, gi * gwid:(gi + 1) * gwid].astype(F32)
                for hh in range(gh):
                    vz_ref[kb, gi, hh * tq:(hh + 1) * tq, :] = (
                        jnp.where(head_of_lane == hh, v, 0.0).astype(BF16))

    qn = (_head_rms(q_ref[...].astype(F32), qw_ref[...]) * (dh ** -0.5 * LOG2E)).astype(BF16)
    w2 = w2_ref[...]
    row = lax.broadcasted_iota(jnp.int32, (tq, tq), 0)
    col = lax.broadcasted_iota(jnp.int32, (tq, tq), 1)
    causal = col < row

    def logits(kb):
        kt = kt_ref[kb]
        return [_dot(qn[:, hh * dh:(hh + 1) * dh], kt[hh * dh:(hh + 1) * dh, :]) for hh in range(nh)]

    def stage_logits(zs, diag):
        rs = []
        for hh in range(nh):
            z = zs[hh]
            sp = jnp.maximum(z, 0.0) + jnp.log2(1.0 + jnp.exp2(_neg_abs(z)))
            spm = jnp.where(causal, sp, 0.0) if diag else sp
            rem = _dot(spm.astype(BF16), w2)
            rs.append((z - sp, rem, jnp.sum(spm, axis=-1, keepdims=True)))
        for hh in range(nh):
            base, rem, tot = rs[hh]
            lw = base + rem
            lw_ref[hh] = jnp.where(causal, lw, -jnp.inf) if diag else lw
            tot_ref[hh] = jnp.broadcast_to(-tot, (tq, sub))

    def weights():
        ws = []
        for hh in range(nh):
            c = carry_ref[hh]
            lw = lw_ref[hh]
            w = jnp.exp2(jnp.concatenate([lw[:, :sub] + c, lw[:, sub:] + c], axis=1))
            carry_ref[hh] = c + tot_ref[hh]
            ws.append(w.astype(BF16))
        return ws

    def pv_dot(ws, kb):
        return jnp.concatenate(
            [_dot(jnp.concatenate(ws[gi * gh:(gi + 1) * gh], axis=1), vz_ref[kb, gi])
             for gi in range(nh // gh)], axis=1)

    carry_ref[...] = jnp.zeros(carry_ref.shape, F32)
    acc_ref[...] = jnp.zeros(acc_ref.shape, F32)
    stage_logits(logits(qi), True)

    def step(j):
        zs = logits(qi - 1 - j)
        pv = pv_dot(weights(), qi - j)
        stage_logits(zs, False)
        acc_ref[...] += pv

    odd = qi & 1

    @pl.when(odd == 1)
    def _():
        step(0)

    def body(i, c):
        j = odd + 2 * i
        step(j)
        step(j + 1)
        return c

    lax.fori_loop(0, lax.shift_right_logical(qi, 1), body, 0)
    acc_ref[...] += pv_dot(weights(), 0)
    o_ref[...] = acc_ref[...].astype(o_ref.dtype)


def _stick_breaking(p0, q_norm_w, k_norm_w, *, bsz, seq):
    tq = SB_TILE
    sub = CHUNK
    nh = SB_HEADS_PER_STEP
    width = nh * SB_HEAD_DIM
    nq = seq // tq
    w2 = jnp.where(jnp.arange(tq)[:, None] > jnp.arange(tq)[None, :], -1.0, 0.0).astype(BF16)
    return pl.pallas_call(
        functools.partial(_sb_kernel, nkb=nq),
        grid=(bsz, SB_HEADS // nh, nq),
        in_specs=[
            pl.BlockSpec((tq, width), lambda b, h, i: (b * nq + i, P0_Q // width + h)),
            pl.BlockSpec((seq, width), lambda b, h, i: (b, P0_K // width + h)),
            pl.BlockSpec((seq, width), lambda b, h, i: (b, P0_V // width + h)),
            pl.BlockSpec((1, width), lambda b, h, i: (0, 0)),
            pl.BlockSpec((1, width), lambda b, h, i: (0, 0)),
            pl.BlockSpec((tq, tq), lambda b, h, i: (0, 0)),
        ],
        out_specs=pl.BlockSpec((tq, width), lambda b, h, i: (b * nq + i, h)),
        out_shape=jax.ShapeDtypeStruct((bsz * seq, SB_INNER), BF16),
        scratch_shapes=[
            pltpu.VMEM((nq, width, tq), BF16),
            pltpu.VMEM((nq, nh // SB_PV_HEADS, SB_PV_HEADS * tq, SB_PV_HEADS * SB_HEAD_DIM), BF16),
            pltpu.VMEM((nh, tq, sub), F32),
            pltpu.VMEM((tq, width), F32),
            pltpu.VMEM((nh, tq, tq), F32),
            pltpu.VMEM((nh, tq, sub), F32),
        ],
        compiler_params=pltpu.CompilerParams(
            dimension_semantics=("parallel", "parallel", "arbitrary"), vmem_limit_bytes=VMEM_LIMIT),
        name="stick_breaking",
    )(p0, p0, p0, q_norm_w, k_norm_w, w2)


def _out_proj_kernel(*refs, n_in):
    a_refs = refs[:n_in]
    w_refs = refs[n_in:2 * n_in]
    x_ref = refs[2 * n_in]
    o_ref = refs[2 * n_in + 1]
    acc = x_ref[...]
    for a_ref, w_ref in zip(a_refs, w_refs):
        acc = acc + _dot(a_ref[...], w_ref[...])
    o_ref[...] = acc


def _out_proj(acts, weights, x, *, tm):
    t, d = x.shape
    n_in = len(acts)
    in_specs = [pl.BlockSpec((tm, a.shape[1]), lambda i: (i, 0)) for a in acts]
    in_specs += [_resident(w.shape) for w in weights]
    in_specs += [pl.BlockSpec((tm, d), lambda i: (i, 0))]
    return pl.pallas_call(
        functools.partial(_out_proj_kernel, n_in=n_in),
        grid=(t // tm,),
        in_specs=in_specs,
        out_specs=pl.BlockSpec((tm, d), lambda i: (i, 0)),
        out_shape=jax.ShapeDtypeStruct((t, d), F32),
        compiler_params=pltpu.CompilerParams(
            dimension_semantics=("parallel",), vmem_limit_bytes=VMEM_LIMIT),
        name="out_proj",
    )(*acts, *weights, x)


def _ffn_kernel(h_ref, halo_ref, nw_ref, upw_ref, cw_ref, cb_ref, dw_ref, o_ref, *, tm, seq, cw):
    i = pl.program_id(0)
    hr = BF16_ROWS
    h = h_ref[...]
    keep = jnp.where(lax.rem(i * tm, seq) == 0, 0.0, 1.0)
    hn_ext = jnp.concatenate(
        [_rms(halo_ref[...], nw_ref[...]) * keep, _rms(h, nw_ref[...])], axis=0).astype(BF16)
    acc = h
    for c in range(D_FF // cw):
        ys = []
        for base in (0, D_FF):
            sl = slice(base + c * cw, base + (c + 1) * cw)
            u = _dot(hn_ext, upw_ref[:, sl])
            y = cb_ref[:, sl]
            for k in range(FFN_CONV):
                off = hr - (FFN_CONV - 1) + k
                y = y + cw_ref[k:k + 1, sl] * u[off:off + tm, :]
            ys.append(y)
        act = (_silu(ys[0]) * ys[1]).astype(BF16)
        acc = acc + _dot(act, dw_ref[c * cw:(c + 1) * cw, :])
    o_ref[...] = acc


def _conv_ffn(h, nw, up_w, conv_w, conv_b, down_w, *, seq, tm, cw):
    t, d = h.shape
    hr = BF16_ROWS
    const = _resident
    return pl.pallas_call(
        functools.partial(_ffn_kernel, tm=tm, seq=seq, cw=cw),
        grid=(t // tm,),
        in_specs=[
            pl.BlockSpec((tm, d), lambda i: (i, 0)),
            pl.BlockSpec((hr, d), lambda i: (jnp.maximum(i * (tm // hr) - 1, 0), 0)),
            const((1, d)),
            const(up_w.shape),
            const(conv_w.shape),
            const(conv_b.shape),
            const(down_w.shape),
        ],
        out_specs=pl.BlockSpec((tm, d), lambda i: (i, 0)),
        out_shape=jax.ShapeDtypeStruct((t, d), F32),
        compiler_params=pltpu.CompilerParams(
            dimension_semantics=("parallel",), vmem_limit_bytes=VMEM_LIMIT),
        name="conv_ffn",
    )(h, h, nw, up_w, conv_w, conv_b, down_w)


def _mlstm_kernel(q_ref, k_ref, v_ref, og_ref, ig_ref, fg_ref, ib_ref, fb_ref, nw_ref,
                  o_ref, cn_ref, m_ref):
    c = pl.program_id(1)
    l = CHUNK
    kd = ML_QK_DIM
    vd = ML_V_DIM

    @pl.when(c == 0)
    def _():
        cn_ref[...] = jnp.zeros(cn_ref.shape, F32)
        m_ref[...] = jnp.zeros(m_ref.shape, F32)

    cap = lambda x: ML_GATE_CAP * jnp.tanh(x * (1.0 / ML_GATE_CAP))
    logi = cap(ig_ref[...] + ib_ref[...])
    fc = cap(fg_ref[...] + fb_ref[...])
    logf = jnp.minimum(fc, 0.0) - jnp.log1p(jnp.exp(-jnp.abs(fc)))
    tri = _tril_incl(l)
    bcs = _tri_cumsum(tri.astype(BF16), logf)
    bcs_t = bcs.T
    logi_t = logi.T
    glast = bcs[l - 1:l, :]
    m_all = m_ref[...]

    q = q_ref[...]
    k = k_ref[...]
    k_t = k.astype(F32).T.astype(BF16)
    heads = range(ML_HEADS)
    qs = [(q[:, h * kd:(h + 1) * kd].astype(F32) * (kd ** -0.5)).astype(BF16) for h in heads]
    vs = [v_ref[:, h * vd:(h + 1) * vd] for h in heads]
    cns = [cn_ref[h] for h in heads]
    qk = [_dot_nt(qs[h], k[:, h * kd:(h + 1) * kd]) for h in heads]
    qc = [_dot(qs[h], cns[h].astype(BF16)) for h in heads]
    bcol = [bcs[:, h:h + 1] for h in heads]
    m_prev = [m_all[:, h:h + 1] for h in heads]
    dm = [jnp.where(tri, bcol[h] - bcs_t[h:h + 1, :] + logi_t[h:h + 1, :], -jnp.inf) for h in heads]
    inter = [bcol[h] + m_prev[h] for h in heads]
    m_t = [jnp.maximum(inter[h], jnp.max(dm[h], axis=-1, keepdims=True)) for h in heads]
    sw = [jnp.exp(dm[h] - m_t[h]) * qk[h] for h in heads]
    iw = [jnp.exp(inter[h] - m_t[h]) for h in heads]
    num = [_dot(sw[h].astype(BF16), vs[h]) + qc[h][:, 0:vd] * iw[h] for h in heads]
    den = [jnp.sum(sw[h], axis=-1, keepdims=True) + qc[h][:, vd:vd + 1] * iw[h] for h in heads]
    hm = [num[h] / jnp.maximum(jnp.abs(den[h]), jnp.exp(-m_t[h])) for h in heads]
    hm = [_rms(hm[h], nw_ref[:, h * vd:(h + 1) * vd]) for h in heads]
    for h in heads:
        og = og_ref[:, h * vd:(h + 1) * vd].astype(F32)
        o_ref[:, h * vd:(h + 1) * vd] = (hm[h] * _sigmoid(og)).astype(o_ref.dtype)

    g = [glast[:, h:h + 1] for h in heads]
    w_end = [g[h] - bcol[h] + logi[:, h:h + 1] for h in heads]
    m_loc = [jnp.max(w_end[h], axis=0, keepdims=True) for h in heads]
    e_end = [jnp.exp(w_end[h] - m_loc[h]) for h in heads]
    xe = [jnp.concatenate([vs[h].astype(F32) * e_end[h], jnp.broadcast_to(e_end[h], (l, vd))],
                          axis=1).astype(BF16) for h in heads]
    loc = [_dot(k_t[h * kd:(h + 1) * kd, :], xe[h]) for h in heads]
    for h in heads:
        m_new = jnp.maximum(g[h] + m_prev[h], m_loc[h])
        cn_ref[h] = jnp.exp(g[h] + m_prev[h] - m_new) * cns[h] + jnp.exp(m_loc[h] - m_new) * loc[h]
        m_ref[:, h:h + 1] = m_new


def _mlstm(p1, gates, i_b, f_b, norm_w, *, bsz, seq):
    nc = seq // CHUNK
    l = CHUNK
    row = lambda b, c: b * nc + c
    full = lambda shape: pl.BlockSpec(shape, lambda b, c: (0, 0))
    return pl.pallas_call(
        _mlstm_kernel,
        grid=(bsz, nc),
        in_specs=[
            pl.BlockSpec((l, ML_QK), lambda b, c: (row(b, c), 0)),
            pl.BlockSpec((l, ML_QK), lambda b, c: (row(b, c), 1)),
            pl.BlockSpec((l, ML_V), lambda b, c: (row(b, c), 1)),
            pl.BlockSpec((l, ML_V), lambda b, c: (row(b, c), 2)),
            pl.BlockSpec((l, LANES), lambda b, c: (row(b, c), 0)),
            pl.BlockSpec((l, LANES), lambda b, c: (row(b, c), 1)),
            full((1, LANES)),
            full((1, LANES)),
            full((1, ML_V)),
        ],
        out_specs=pl.BlockSpec((l, ML_V), lambda b, c: (row(b, c), 0)),
        out_shape=jax.ShapeDtypeStruct((bsz * seq, ML_V), BF16),
        scratch_shapes=[
            pltpu.VMEM((ML_HEADS, ML_QK_DIM, 2 * ML_V_DIM), F32),
            pltpu.VMEM((1, LANES), F32),
        ],
        compiler_params=pltpu.CompilerParams(
            dimension_semantics=("parallel", "arbitrary"), vmem_limit_bytes=VMEM_LIMIT),
        name="mlstm_scan",
    )(p1, p1, p1, p1, gates, gates, i_b, f_b, norm_w)


def _pad_cols(w, n):
    return jnp.pad(w, ((0, 0), (0, n - w.shape[1])))


def _row(v, n=None):
    v = v.reshape(1, -1).astype(F32)
    return v if n is None else _pad_cols(v, n)


def kernel(x, norm_w, hy_in_w, ssd_conv_w, ssd_conv_b, ssd_dt_bias, ssd_a_log, ssd_d, ssd_norm_w,
           sb_q_norm_w, sb_k_norm_w, hy_out_w, ml_in_w, ml_i_b, ml_f_b, ml_norm_w, ml_out_w,
           ffn_up_w, ffn_conv_w, ffn_conv_b, ffn_down_w):
    bsz, seq, d = x.shape
    t = bsz * seq
    tm = min(512, seq)
    h = x.reshape(t, d)

    w = hy_in_w[0]
    o_xbc = SSD_INNER
    o_dt = o_xbc + SSD_INNER + SSD_BC
    o_q = o_dt + SSD_HEADS
    w_z = w[:, 0:SSD_INNER]
    w_x = w[:, o_xbc:o_xbc + SSD_INNER]
    w_bc = w[:, o_xbc + SSD_INNER:o_dt]
    w_dt = w[:, o_dt:o_q]
    w_qkv = w[:, o_q:]
    w_main = jnp.concatenate([w_z, w_qkv, w_x, w_bc], axis=1).astype(BF16)
    w_small = _pad_cols(w_dt, LANES).astype(BF16)
    p0, dt_raw = _norm_proj(h, _row(norm_w[0, 0]), w_main, w_small, tm=tm, tn=512)

    y_ssd = _ssd(p0, dt_raw, ssd_conv_w[0], _row(ssd_conv_b[0]), _row(ssd_dt_bias[0], LANES),
                 _row(ssd_a_log[0], LANES), _row(jnp.repeat(ssd_d[0], SSD_HEAD_DIM)),
                 _row(ssd_norm_w[0]), bsz=bsz, seq=seq)
    y_sb = _stick_breaking(p0, _row(jnp.tile(sb_q_norm_w[0], SB_HEADS_PER_STEP)),
                           _row(jnp.tile(sb_k_norm_w[0], SB_HEADS_PER_STEP)),
                           bsz=bsz, seq=seq)
    w_out = hy_out_w[0].astype(BF16)
    h = _out_proj([y_ssd, y_sb], [w_out[0:SSD_INNER], w_out[SSD_INNER:]], h, tm=tm)
    h = _conv_ffn(h, _row(norm_w[0, 1]), ffn_up_w[0].astype(BF16), ffn_conv_w[0], _row(ffn_conv_b[0]),
                  ffn_down_w[0].astype(BF16), seq=seq, tm=tm, cw=D_FF // 2)

    w = ml_in_w[0]
    o_g = 2 * ML_QK + 2 * ML_V
    w_main = w[:, 0:o_g].astype(BF16)
    w_small = jnp.concatenate(
        [_pad_cols(w[:, o_g:o_g + ML_HEADS], LANES), _pad_cols(w[:, o_g + ML_HEADS:], LANES)],
        axis=1).astype(BF16)
    p1, gates = _norm_proj(h, _row(norm_w[1, 0]), w_main, w_small, tm=tm, tn=512)
    y_ml = _mlstm(p1, gates, _row(ml_i_b[0], LANES), _row(ml_f_b[0], LANES), _row(ml_norm_w[0]),
                  bsz=bsz, seq=seq)
    h = _out_proj([y_ml], [ml_out_w[0].astype(BF16)], h, tm=tm)
    h = _conv_ffn(h, _row(norm_w[1, 1]), ffn_up_w[1].astype(BF16), ffn_conv_w[1], _row(ffn_conv_b[1]),
                  ffn_down_w[1].astype(BF16), seq=seq, tm=tm, cw=D_FF // 2)
    return h.reshape(bsz, seq, d)
```

```python
import functools

import jax
import jax.numpy as jnp
from jax import lax
from jax.experimental import pallas as pl
from jax.experimental.pallas import tpu as pltpu

F32 = jnp.float32
BF16 = jnp.bfloat16

D_MODEL = 1024
SSD_HEADS = 16
SSD_HEAD_DIM = 64
SSD_INNER = SSD_HEADS * SSD_HEAD_DIM
SSD_GROUPS = 2
SSD_STATE = 128
SSD_CONV = 4
SSD_BC = 2 * SSD_GROUPS * SSD_STATE
SB_HEADS = 16
SB_HEAD_DIM = 64
SB_INNER = SB_HEADS * SB_HEAD_DIM
ML_HEADS = 8
ML_QK_DIM = 64
ML_V_DIM = 128
ML_QK = ML_HEADS * ML_QK_DIM
ML_V = ML_HEADS * ML_V_DIM
ML_GATE_CAP = 15.0
D_FF = 2816
FFN_CONV = 3
RMS_EPS = 1e-6

CHUNK = 128
SB_TILE = 256
LOG2E = 1.4426950408889634
SB_HEADS_PER_STEP = 8
SB_PV_HEADS = 4
LANES = 128
BF16_ROWS = 16
VMEM_LIMIT = 56 * 1024 * 1024

P0_Z, P0_Q, P0_K, P0_V, P0_X, P0_BC = 0, 1024, 2048, 3072, 4096, 5120
P0_COLS = 5632
P1_COLS = 3072


def _sigmoid(x):
    return 1.0 / (1.0 + jnp.exp(-x))


def _silu(x):
    return x * _sigmoid(x)


def _softplus(x):
    return jnp.maximum(x, 0.0) + jnp.log1p(jnp.exp(-jnp.abs(x)))


def _rms(x, w):
    ms = jnp.mean(x * x, axis=-1, keepdims=True)
    return x * lax.rsqrt(ms + RMS_EPS) * w


def _neg_abs(x):
    bits = pltpu.bitcast(x, jnp.uint32) | jnp.uint32(0x80000000)
    return pltpu.bitcast(bits, F32)


def _dot(a, b):
    return jnp.dot(a, b, preferred_element_type=F32)


def _dot_nt(a, b):
    return lax.dot_general(a, b, (((1,), (1,)), ((), ())), preferred_element_type=F32)


def _split3(v):
    hi = v.astype(BF16)
    r1 = v - hi.astype(F32)
    mid = r1.astype(BF16)
    lo = (r1 - mid.astype(F32)).astype(BF16)
    return hi, mid, lo


def _tri_cumsum(tri, v):
    hi, mid, lo = _split3(v)
    return _dot(tri, hi) + _dot(tri, mid) + _dot(tri, lo)


def _tril_incl(n):
    r = lax.broadcasted_iota(jnp.int32, (n, n), 0)
    c = lax.broadcasted_iota(jnp.int32, (n, n), 1)
    return c <= r


def _norm_proj_kernel(x_ref, nw_ref, w_ref, ws_ref, o_ref, os_ref, *, tn):
    xn = _rms(x_ref[...], nw_ref[...]).astype(BF16)
    os_ref[...] = _dot(xn, ws_ref[...])
    for j in range(w_ref.shape[1] // tn):
        cols = slice(j * tn, (j + 1) * tn)
        o_ref[:, cols] = _dot(xn, w_ref[:, cols]).astype(o_ref.dtype)


def _resident(shape):
    return pl.BlockSpec(shape, lambda *_: (0,) * len(shape), pipeline_mode=pl.Buffered(1))


def _norm_proj(x, nw, w_main, w_small, *, tm, tn):
    t, d = x.shape
    n = w_main.shape[1]
    ns = w_small.shape[1]
    return pl.pallas_call(
        functools.partial(_norm_proj_kernel, tn=tn),
        grid=(t // tm,),
        in_specs=[
            pl.BlockSpec((tm, d), lambda i: (i, 0)),
            _resident((1, d)),
            _resident((d, n)),
            _resident((d, ns)),
        ],
        out_specs=[
            pl.BlockSpec((tm, n), lambda i: (i, 0)),
            pl.BlockSpec((tm, ns), lambda i: (i, 0)),
        ],
        out_shape=[jax.ShapeDtypeStruct((t, n), BF16), jax.ShapeDtypeStruct((t, ns), F32)],
        compiler_params=pltpu.CompilerParams(
            dimension_semantics=("parallel",), vmem_limit_bytes=VMEM_LIMIT),
        name="norm_proj",
    )(x, nw, w_main, w_small)


def _ssd_kernel(z_ref, x_ref, bc_ref, xp_ref, bcp_ref, dt_ref, cw_ref, cb_ref, dtb_ref, alog_ref, dsk_ref,
                nw_ref, ex_ref, sh_ref, o_ref, state_ref):
    c = pl.program_id(1)
    l = CHUNK
    p = SSD_HEAD_DIM
    hpg = SSD_HEADS // SSD_GROUPS

    @pl.when(c == 0)
    def _():
        state_ref[...] = jnp.zeros(state_ref.shape, F32)

    cur = jnp.concatenate([x_ref[...], bc_ref[...]], axis=1)
    prev = jnp.concatenate([xp_ref[...], bcp_ref[...]], axis=1)
    prev = jnp.where(c > 0, prev, jnp.zeros_like(prev))
    xin = jnp.concatenate([prev, cur], axis=0)
    conv = cb_ref[...] + cw_ref[SSD_CONV - 1:SSD_CONV, :] * cur.astype(F32)
    for k in range(SSD_CONV - 1):
        conv = conv + cw_ref[k:k + 1, :] * _dot(sh_ref[k], xin)
    xbc = _silu(conv)
    xs = xbc[:, 0:SSD_INNER]

    dt = _softplus(dt_ref[...] + dtb_ref[...])
    da = dt * (-jnp.exp(alog_ref[...]))
    tri = _tril_incl(l)
    acs = _tri_cumsum(tri.astype(BF16), da)
    acs_t = acs.T
    last = acs[l - 1:l, :]
    eacs = jnp.exp(acs)
    dtd = dt * jnp.exp(last - acs)

    def expand(v):
        hi = v.astype(BF16)
        lo = (v - hi.astype(F32)).astype(BF16)
        return _dot(jnp.concatenate([hi, lo], axis=1), ex_ref[...])

    dt_x = expand(dt)
    dtd_x = expand(dtd)
    eacs_x = expand(eacs)
    cdec_x = eacs_x[l - 1:l, :]
    xd = xs * dt_x
    xdd = (xs * dtd_x).astype(BF16)

    lane = lax.broadcasted_iota(jnp.int32, (l, LANES), 1)
    first = lane < p
    gw = hpg * p
    y_parts = []
    for g in range(SSD_GROUPS):
        bm = xbc[:, SSD_INNER + g * SSD_STATE:SSD_INNER + (g + 1) * SSD_STATE]
        cm = xbc[:, SSD_INNER + (SSD_GROUPS + g) * SSD_STATE:SSD_INNER + (SSD_GROUPS + g + 1) * SSD_STATE]
        cm_b = cm.astype(BF16)
        cb = _dot_nt(cm_b, bm.astype(BF16))
        cols = slice(g * gw, (g + 1) * gw)
        st = state_ref[g]
        y_off = _dot(cm_b, st.astype(BF16)) * eacs_x[:, cols]
        state_ref[g] = cdec_x[:, cols] * st + _dot(bm.T.astype(BF16), xdd[:, cols])
        for j in range(hpg // 2):
            h0 = g * hpg + 2 * j
            ms = []
            for h in (h0, h0 + 1):
                seg = acs[:, h:h + 1] - acs_t[h:h + 1, :]
                ms.append((cb * jnp.exp(jnp.where(tri, seg, -jnp.inf))).astype(BF16))
            xp = xd[:, h0 * p:(h0 + 2) * p]
            rhs = jnp.concatenate([jnp.where(first, xp, 0.0), jnp.where(first, 0.0, xp)], axis=0)
            y_parts.append(_dot(jnp.concatenate(ms, axis=1), rhs.astype(BF16))
                           + y_off[:, 2 * j * p:(2 * j + 2) * p])

    y = jnp.concatenate(y_parts, axis=1) + xs * dsk_ref[...]
    y = y * _silu(z_ref[...].astype(F32))
    o_ref[...] = _rms(y, nw_ref[...]).astype(o_ref.dtype)


def _ssd(p0, dt_raw, conv_w, conv_b, dt_bias, a_log, d_skip, norm_w, *, bsz, seq):
    nc = seq // CHUNK
    l = CHUNK
    row = lambda b, c: b * nc + c
    prev_row = lambda b, c: jnp.maximum(b * nc + c - 1, 0)
    full = lambda shape: pl.BlockSpec(shape, lambda b, c: (0, 0))
    head_of_col =jnp.arange(SSD_INNER)[None, :] // SSD_HEAD_DIM
    expand = (jnp.arange(2 * LANES)[:, None] % LANES == head_of_col).astype(BF16)
    delay = (SSD_CONV - 1) - jnp.arange(SSD_CONV - 1)[:, None, None]
    shifts = (jnp.arange(2 * l)[None, None, :] == l + jnp.arange(l)[None, :, None] - delay).astype(BF16)
    return pl.pallas_call(
        _ssd_kernel,
        grid=(bsz, nc),
        in_specs=[
            pl.BlockSpec((l, SSD_INNER), lambda b, c: (row(b, c), P0_Z // SSD_INNER)),
            pl.BlockSpec((l, SSD_INNER), lambda b, c: (row(b, c), P0_X // SSD_INNER)),
            pl.BlockSpec((l, SSD_BC), lambda b, c: (row(b, c), P0_BC // SSD_BC)),
            pl.BlockSpec((l, SSD_INNER), lambda b, c: (prev_row(b, c), P0_X // SSD_INNER)),
            pl.BlockSpec((l, SSD_BC), lambda b, c: (prev_row(b, c), P0_BC // SSD_BC)),
            pl.BlockSpec((l, LANES), lambda b, c: (row(b, c), 0)),
            full((SSD_CONV, SSD_INNER + SSD_BC)),
            full((1, SSD_INNER + SSD_BC)),
            full((1, LANES)),
            full((1, LANES)),
            full((1, SSD_INNER)),
            full((1, SSD_INNER)),
            full((2 * LANES, SSD_INNER)),
            pl.BlockSpec((SSD_CONV - 1, l, 2 * l), lambda b, c: (0, 0, 0)),
        ],
        out_specs=pl.BlockSpec((l, SSD_INNER), lambda b, c: (row(b, c), 0)),
        out_shape=jax.ShapeDtypeStruct((bsz * seq, SSD_INNER), BF16),
        scratch_shapes=[
            pltpu.VMEM((SSD_GROUPS, SSD_STATE, SSD_INNER // SSD_GROUPS), F32),
        ],
        compiler_params=pltpu.CompilerParams(
            dimension_semantics=("parallel", "arbitrary"), vmem_limit_bytes=VMEM_LIMIT),
        name="ssd_scan",
    )(p0, p0, p0, p0, p0, dt_raw, conv_w, conv_b, dt_bias, a_log, d_skip, norm_w, expand, shifts)


def _head_rms(x, w):
    dh = SB_HEAD_DIM
    lane = lax.broadcasted_iota(jnp.int32, (x.shape[0], LANES), 1)
    lo = lane < dh
    outs = []
    for s in range(x.shape[1] // LANES):
        xs = x[:, s * LANES:(s + 1) * LANES]
        x2 = xs * xs
        s0 = jnp.sum(jnp.where(lo, x2, 0.0), axis=-1, keepdims=True)
        s1 = jnp.sum(jnp.where(lo, 0.0, x2), axis=-1, keepdims=True)
        ms = jnp.where(lo, s0, s1) * (1.0 / dh)
        outs.append(xs * lax.rsqrt(ms + RMS_EPS))
    return jnp.concatenate(outs, axis=1) * w


def _sb_kernel(q_ref, k_ref, v_ref, qw_ref, kw_ref, w2_ref, o_ref, kt_ref, vz_ref, carry_ref, acc_ref,
               lw_ref, tot_ref, *, nkb):
    qi = pl.program_id(2)
    tq = SB_TILE
    sub = CHUNK
    dh = SB_HEAD_DIM
    nh = SB_HEADS_PER_STEP
    width = nh * dh
    gh = SB_PV_HEADS
    gwid = gh * dh

    @pl.when(qi == 0)
    def _():
        head_of_lane = lax.broadcasted_iota(jnp.int32, (tq, gwid), 1) // dh
        for kb in range(nkb):
            rows = slice(kb * tq, (kb + 1) * tq)
            kn = _head_rms(k_ref[rows, :].astype(F32), kw_ref[...])
            kt_ref[kb] = kn.T.astype(BF16)
            for gi in range(nh // gh):
                v = v_ref[rows, gi * gwid:(gi + 1) * gwid].astype(F32)
                for hh in range(gh):
                    vz_ref[kb, gi, hh * tq:(hh + 1) * tq, :] = (
                        jnp.where(head_of_lane == hh, v, 0.0).astype(BF16))

    qn = (_head_rms(q_ref[...].astype(F32), qw_ref[...]) * (dh ** -0.5 * LOG2E)).astype(BF16)
    w2 = w2_ref[...]
    row = lax.broadcasted_iota(jnp.int32, (tq, tq), 0)
    col = lax.broadcasted_iota(jnp.int32, (tq, tq), 1)
    causal = col < row

    def logits(kb):
        kt = kt_ref[kb]
        return [_dot(qn[:, hh * dh:(hh + 1) * dh], kt[hh * dh:(hh + 1) * dh, :]) for hh in range(nh)]

    def stage_logits(zs, diag):
        rs = []
        for hh in range(nh):
            z = zs[hh]
            sp = jnp.maximum(z, 0.0) + jnp.log2(1.0 + jnp.exp2(_neg_abs(z)))
            spm = jnp.where(causal, sp, 0.0) if diag else sp
            rem = _dot(spm.astype(BF16), w2)
            rs.append((z - sp, rem, jnp.sum(spm, axis=-1, keepdims=True)))
        for hh in range(nh):
            base, rem, tot = rs[hh]
            lw = base + rem
            lw_ref[hh] = jnp.where(causal, lw, -jnp.inf) if diag else lw
            tot_ref[hh] = jnp.broadcast_to(-tot, (tq, sub))

    def weights():
        ws = []
        for hh in range(nh):
            c = carry_ref[hh]
            lw = lw_ref[hh]
            w = jnp.exp2(jnp.concatenate([lw[:, :sub] + c, lw[:, sub:] + c], axis=1))
            carry_ref[hh] = c + tot_ref[hh]
            ws.append(w.astype(BF16))
        return ws

    def pv_dot(ws, kb):
        return jnp.concatenate(
            [_dot(jnp.concatenate(ws[gi * gh:(gi + 1) * gh], axis=1), vz_ref[kb, gi])
             for gi in range(nh // gh)], axis=1)

    carry_ref[...] = jnp.zeros(carry_ref.shape, F32)
    acc_ref[...] = jnp.zeros(acc_ref.shape, F32)
    stage_logits(logits(qi), True)

    def step(j):
        zs = logits(qi - 1 - j)
        pv = pv_dot(weights(), qi - j)
        stage_logits(zs, False)
        acc_ref[...] += pv

    odd = qi & 1

    @pl.when(odd == 1)
    def _():
        step(0)

    def body(i, c):
        j = odd + 2 * i
        step(j)
        step(j + 1)
        return c

    lax.fori_loop(0, lax.shift_right_logical(qi, 1), body, 0)
    acc_ref[...] += pv_dot(weights(), 0)
    o_ref[...] = acc_ref[...].astype(o_ref.dtype)


def _stick_breaking(p0, q_norm_w, k_norm_w, *, bsz, seq):
    tq = SB_TILE
    sub = CHUNK
    nh = SB_HEADS_PER_STEP
    width = nh * SB_HEAD_DIM
    nq = seq // tq
    w2 = jnp.where(jnp.arange(tq)[:, None] > jnp.arange(tq)[None, :], -1.0, 0.0).astype(BF16)
    return pl.pallas_call(
        functools.partial(_sb_kernel, nkb=nq),
        grid=(bsz, SB_HEADS // nh, nq),
        in_specs=[
            pl.BlockSpec((tq, width), lambda b, h, i: (b * nq + i, P0_Q // width + h)),
            pl.BlockSpec((seq, width), lambda b, h, i: (b, P0_K // width + h)),
            pl.BlockSpec((seq, width), lambda b, h, i: (b, P0_V // width + h)),
            pl.BlockSpec((1, width), lambda b, h, i: (0, 0)),
            pl.BlockSpec((1, width), lambda b, h, i: (0, 0)),
            pl.BlockSpec((tq, tq), lambda b, h, i: (0, 0)),
        ],
        out_specs=pl.BlockSpec((tq, width), lambda b, h, i: (b * nq + i, h)),
        out_shape=jax.ShapeDtypeStruct((bsz * seq, SB_INNER), BF16),
        scratch_shapes=[
            pltpu.VMEM((nq, width, tq), BF16),
            pltpu.VMEM((nq, nh // SB_PV_HEADS, SB_PV_HEADS * tq, SB_PV_HEADS * SB_HEAD_DIM), BF16),
            pltpu.VMEM((nh, tq, sub), F32),
            pltpu.VMEM((tq, width), F32),
            pltpu.VMEM((nh, tq, tq), F32),
            pltpu.VMEM((nh, tq, sub), F32),
        ],
        compiler_params=pltpu.CompilerParams(
            dimension_semantics=("parallel", "parallel", "arbitrary"), vmem_limit_bytes=VMEM_LIMIT),
        name="stick_breaking",
    )(p0, p0, p0, q_norm_w, k_norm_w, w2)


def _out_proj_kernel(*refs, n_in):
    a_refs = refs[:n_in]
    w_refs = refs[n_in:2 * n_in]
    x_ref = refs[2 * n_in]
    o_ref = refs[2 * n_in + 1]
    acc = x_ref[...]
    for a_ref, w_ref in zip(a_refs, w_refs):
        acc = acc + _dot(a_ref[...], w_ref[...])
    o_ref[...] = acc


def _out_proj(acts, weights, x, *, tm):
    t, d = x.shape
    n_in = len(acts)
    in_specs = [pl.BlockSpec((tm, a.shape[1]), lambda i: (i, 0)) for a in acts]
    in_specs += [_resident(w.shape) for w in weights]
    in_specs += [pl.BlockSpec((tm, d), lambda i: (i, 0))]
    return pl.pallas_call(
        functools.partial(_out_proj_kernel, n_in=n_in),
        grid=(t // tm,),
        in_specs=in_specs,
        out_specs=pl.BlockSpec((tm, d), lambda i: (i, 0)),
        out_shape=jax.ShapeDtypeStruct((t, d), F32),
        compiler_params=pltpu.CompilerParams(
            dimension_semantics=("parallel",), vmem_limit_bytes=VMEM_LIMIT),
        name="out_proj",
    )(*acts, *weights, x)


def _ffn_kernel(h_ref, halo_ref, nw_ref, upw_ref, cw_ref, cb_ref, dw_ref, o_ref, *, tm, seq, cw):
    i = pl.program_id(0)
    hr = BF16_ROWS
    h = h_ref[...]
    keep = jnp.where(lax.rem(i * tm, seq) == 0, 0.0, 1.0)
    hn_ext = jnp.concatenate(
        [_rms(halo_ref[...], nw_ref[...]) * keep, _rms(h, nw_ref[...])], axis=0).astype(BF16)
    acc = h
    for c in range(D_FF // cw):
        ys = []
        for base in (0, D_FF):
            sl = slice(base + c * cw, base + (c + 1) * cw)
            u = _dot(hn_ext, upw_ref[:, sl])
            y = cb_ref[:, sl]
            for k in range(FFN_CONV):
                off = hr - (FFN_CONV - 1) + k
                y = y + cw_ref[k:k + 1, sl] * u[off:off + tm, :]
            ys.append(y)
        act = (_silu(ys[0]) * ys[1]).astype(BF16)
        acc = acc + _dot(act, dw_ref[c * cw:(c + 1) * cw, :])
    o_ref[...] = acc


def _conv_ffn(h, nw, up_w, conv_w, conv_b, down_w, *, seq, tm, cw):
    t, d = h.shape
    hr = BF16_ROWS
    const = _resident
    return pl.pallas_call(
        functools.partial(_ffn_kernel, tm=tm, seq=seq, cw=cw),
        grid=(t // tm,),
        in_specs=[
            pl.BlockSpec((tm, d), lambda i: (i, 0)),
            pl.BlockSpec((hr, d), lambda i: (jnp.maximum(i * (tm // hr) - 1, 0), 0)),
            const((1, d)),
            const(up_w.shape),
            const(conv_w.shape),
            const(conv_b.shape),
            const(down_w.shape),
        ],
        out_specs=pl.BlockSpec((tm, d), lambda i: (i, 0)),
        out_shape=jax.ShapeDtypeStruct((t, d), F32),
        compiler_params=pltpu.CompilerParams(
            dimension_semantics=("parallel",), vmem_limit_bytes=VMEM_LIMIT),
        name="conv_ffn",
    )(h, h, nw, up_w, conv_w, conv_b, down_w)


def _mlstm_kernel(q_ref, k_ref, v_ref, og_ref, ig_ref, fg_ref, ib_ref, fb_ref, nw_ref,
                  o_ref, cn_ref, m_ref):
    c = pl.program_id(1)
    l = CHUNK
    kd = ML_QK_DIM
    vd = ML_V_DIM

    @pl.when(c == 0)
    def _():
        cn_ref[...] = jnp.zeros(cn_ref.shape, F32)
        m_ref[...] = jnp.zeros(m_ref.shape, F32)

    cap = lambda x: ML_GATE_CAP * jnp.tanh(x * (1.0 / ML_GATE_CAP))
    logi = cap(ig_ref[...] + ib_ref[...])
    fc = cap(fg_ref[...] + fb_ref[...])
    logf = jnp.minimum(fc, 0.0) - jnp.log1p(jnp.exp(-jnp.abs(fc)))
    tri = _tril_incl(l)
    bcs = _tri_cumsum(tri.astype(BF16), logf)
    m_prev = m_ref[...]

    cm = logi - bcs
    pm = cm
    k = 1
    while k < l:
        pm = jnp.maximum(pm, jnp.concatenate([jnp.full((k, LANES), -jnp.inf, F32), pm[:l - k]], axis=0))
        k *= 2
    mx = jnp.maximum(m_prev, pm)
    u = -mx
    iw = jnp.exp(m_prev - mx)
    emt = jnp.exp(-(bcs + mx))
    glast = bcs[l - 1:l, :]
    w_end = glast + cm
    m_loc = jnp.max(w_end, axis=0, keepdims=True)
    e_end = jnp.exp(w_end - m_loc)
    m_new = jnp.maximum(glast + m_prev, m_loc)
    a_old = jnp.exp(glast + m_prev - m_new)
    a_loc = jnp.exp(m_loc - m_new)
    m_ref[...] = m_new
    cm_t = cm.T

    q = q_ref[...]
    k_all = k_ref[...]
    k_t = k_all.astype(F32).T.astype(BF16)
    ones = jnp.ones((l, vd), BF16)
    heads = range(ML_HEADS)
    col = lambda a, h: jnp.broadcast_to(a[:, h:h + 1], (l, vd))
    qs = [(q[:, h * kd:(h + 1) * kd].astype(F32) * (kd ** -0.5)).astype(BF16) for h in heads]
    vs = [v_ref[:, h * vd:(h + 1) * vd] for h in heads]
    cns = [cn_ref[h] for h in heads]
    qk = [_dot_nt(qs[h], k_all[:, h * kd:(h + 1) * kd]) for h in heads]
    qc = [_dot(qs[h], cns[h].astype(BF16)) for h in heads]
    sw = [jnp.where(tri, jnp.exp(col(u, h) + cm_t[h:h + 1, :]), 0.0) * qk[h] for h in heads]
    nd = [_dot(sw[h].astype(BF16), jnp.concatenate([vs[h], ones], axis=1)) for h in heads]
    iwc = [col(iw, h) for h in heads]
    num = [nd[h][:, 0:vd] + qc[h][:, 0:vd] * iwc[h] for h in heads]
    den = [nd[h][:, vd:] + qc[h][:, vd:] * iwc[h] for h in heads]
    hm = [num[h] / jnp.maximum(jnp.abs(den[h]), col(emt, h)) for h in heads]
    hm = [_rms(hm[h], nw_ref[:, h * vd:(h + 1) * vd]) for h in heads]
    for h in heads:
        og = og_ref[:, h * vd:(h + 1) * vd].astype(F32)
        o_ref[:, h * vd:(h + 1) * vd] = (hm[h] * _sigmoid(og)).astype(o_ref.dtype)

    ec = [col(e_end, h) for h in heads]
    xe = [jnp.concatenate([vs[h].astype(F32) * ec[h], ec[h]], axis=1).astype(BF16) for h in heads]
    loc = [_dot(k_t[h * kd:(h + 1) * kd, :], xe[h]) for h in heads]
    for h in heads:
        cn_ref[h] = a_old[:, h:h + 1] * cns[h] + a_loc[:, h:h + 1] * loc[h]


def _mlstm(p1, gates, i_b, f_b, norm_w, *, bsz, seq):
    nc = seq // CHUNK
    l = CHUNK
    row = lambda b, c: b * nc + c
    full = lambda shape: pl.BlockSpec(shape, lambda b, c: (0, 0))
    return pl.pallas_call(
        _mlstm_kernel,
        grid=(bsz, nc),
        in_specs=[
            pl.BlockSpec((l, ML_QK), lambda b, c: (row(b, c), 0)),
            pl.BlockSpec((l, ML_QK), lambda b, c: (row(b, c), 1)),
            pl.BlockSpec((l, ML_V), lambda b, c: (row(b, c), 1)),
            pl.BlockSpec((l, ML_V), lambda b, c: (row(b, c), 2)),
            pl.BlockSpec((l, LANES), lambda b, c: (row(b, c), 0)),
            pl.BlockSpec((l, LANES), lambda b, c: (row(b, c), 1)),
            full((1, LANES)),
            full((1, LANES)),
            full((1, ML_V)),
        ],
        out_specs=pl.BlockSpec((l, ML_V), lambda b, c: (row(b, c), 0)),
        out_shape=jax.ShapeDtypeStruct((bsz * seq, ML_V), BF16),
        scratch_shapes=[
            pltpu.VMEM((ML_HEADS, ML_QK_DIM, 2 * ML_V_DIM), F32),
            pltpu.VMEM((1, LANES), F32),
        ],
        compiler_params=pltpu.CompilerParams(
            dimension_semantics=("parallel", "arbitrary"), vmem_limit_bytes=VMEM_LIMIT),
        name="mlstm_scan",
    )(p1, p1, p1, p1, gates, gates, i_b, f_b, norm_w)


def _pad_cols(w, n):
    return jnp.pad(w, ((0, 0), (0, n - w.shape[1])))


def _row(v, n=None):
    v = v.reshape(1, -1).astype(F32)
    return v if n is None else _pad_cols(v, n)


def kernel(x, norm_w, hy_in_w, ssd_conv_w, ssd_conv_b, ssd_dt_bias, ssd_a_log, ssd_d, ssd_norm_w,
           sb_q_norm_w, sb_k_norm_w, hy_out_w, ml_in_w, ml_i_b, ml_f_b, ml_norm_w, ml_out_w,
           ffn_up_w, ffn_conv_w, ffn_conv_b, ffn_down_w):
    bsz, seq, d = x.shape
    t = bsz * seq
    tm = min(512, seq)
    h = x.reshape(t, d)

    w = hy_in_w[0]
    o_xbc = SSD_INNER
    o_dt = o_xbc + SSD_INNER + SSD_BC
    o_q = o_dt + SSD_HEADS
    w_z = w[:, 0:SSD_INNER]
    w_x = w[:, o_xbc:o_xbc + SSD_INNER]
    w_bc = w[:, o_xbc + SSD_INNER:o_dt]
    w_dt = w[:, o_dt:o_q]
    w_qkv = w[:, o_q:]
    w_main = jnp.concatenate([w_z, w_qkv, w_x, w_bc], axis=1).astype(BF16)
    w_small = _pad_cols(w_dt, LANES).astype(BF16)
    p0, dt_raw = _norm_proj(h, _row(norm_w[0, 0]), w_main, w_small, tm=tm, tn=512)

    y_ssd = _ssd(p0, dt_raw, ssd_conv_w[0], _row(ssd_conv_b[0]), _row(ssd_dt_bias[0], LANES),
                 _row(ssd_a_log[0], LANES), _row(jnp.repeat(ssd_d[0], SSD_HEAD_DIM)),
                 _row(ssd_norm_w[0]), bsz=bsz, seq=seq)
    y_sb = _stick_breaking(p0, _row(jnp.tile(sb_q_norm_w[0], SB_HEADS_PER_STEP)),
                           _row(jnp.tile(sb_k_norm_w[0], SB_HEADS_PER_STEP)),
                           bsz=bsz, seq=seq)
    w_out = hy_out_w[0].astype(BF16)
    h = _out_proj([y_ssd, y_sb], [w_out[0:SSD_INNER], w_out[SSD_INNER:]], h, tm=tm)
    h = _conv_ffn(h, _row(norm_w[0, 1]), ffn_up_w[0].astype(BF16), ffn_conv_w[0], _row(ffn_conv_b[0]),
                  ffn_down_w[0].astype(BF16), seq=seq, tm=tm, cw=D_FF // 2)

    w = ml_in_w[0]
    o_g = 2 * ML_QK + 2 * ML_V
    w_main = w[:, 0:o_g].astype(BF16)
    w_small = jnp.concatenate(
        [_pad_cols(w[:, o_g:o_g + ML_HEADS], LANES), _pad_cols(w[:, o_g + ML_HEADS:], LANES)],
        axis=1).astype(BF16)
    p1, gates = _norm_proj(h, _row(norm_w[1, 0]), w_main, w_small, tm=tm, tn=512)
    y_ml = _mlstm(p1, gates, _row(ml_i_b[0], LANES), _row(ml_f_b[0], LANES), _row(ml_norm_w[0]),
                  bsz=bsz, seq=seq)
    h = _out_proj([y_ml], [ml_out_w[0].astype(BF16)], h, tm=tm)
    h = _conv_ffn(h, _row(norm_w[1, 1]), ffn_up_w[1].astype(BF16), ffn_conv_w[1], _row(ffn_conv_b[1]),
                  ffn_down_w[1].astype(BF16), seq=seq, tm=tm, cw=D_FF // 2)
    return h.reshape(bsz, seq, d)
```

```python
import functools

import jax
import jax.numpy as jnp
from jax import lax
from jax.experimental import pallas as pl
from jax.experimental.pallas import tpu as pltpu

F32 = jnp.float32
BF16 = jnp.bfloat16

D_MODEL = 1024
SSD_HEADS = 16
SSD_HEAD_DIM = 64
SSD_INNER = SSD_HEADS * SSD_HEAD_DIM
SSD_GROUPS = 2
SSD_STATE = 128
SSD_CONV = 4
SSD_BC = 2 * SSD_GROUPS * SSD_STATE
SB_HEADS = 16
SB_HEAD_DIM = 64
SB_INNER = SB_HEADS * SB_HEAD_DIM
ML_HEADS = 8
ML_QK_DIM = 64
ML_V_DIM = 128
ML_QK = ML_HEADS * ML_QK_DIM
ML_V = ML_HEADS * ML_V_DIM
ML_GATE_CAP = 15.0
D_FF = 2816
FFN_CONV = 3
RMS_EPS = 1e-6

CHUNK = 128
SB_TILE = 256
LOG2E = 1.4426950408889634
SB_HEADS_PER_STEP = 8
SB_PV_HEADS = 4
LANES = 128
BF16_ROWS = 16
VMEM_LIMIT = 56 * 1024 * 1024

P0_Z, P0_Q, P0_K, P0_V, P0_X, P0_BC = 0, 1024, 2048, 3072, 4096, 5120
P0_COLS = 5632
P1_COLS = 3072


def _sigmoid(x):
    return 1.0 / (1.0 + jnp.exp(-x))


def _silu(x):
    return x * _sigmoid(x)


def _softplus(x):
    return jnp.maximum(x, 0.0) + jnp.log1p(jnp.exp(-jnp.abs(x)))


def _rms(x, w):
    ms = jnp.mean(x * x, axis=-1, keepdims=True)
    return x * lax.rsqrt(ms + RMS_EPS) * w


def _neg_abs(x):
    bits = pltpu.bitcast(x, jnp.uint32) | jnp.uint32(0x80000000)
    return pltpu.bitcast(bits, F32)


def _dot(a, b):
    return jnp.dot(a, b, preferred_element_type=F32)


def _dot_nt(a, b):
    return lax.dot_general(a, b, (((1,), (1,)), ((), ())), preferred_element_type=F32)


def _split3(v):
    hi = v.astype(BF16)
    r1 = v - hi.astype(F32)
    mid = r1.astype(BF16)
    lo = (r1 - mid.astype(F32)).astype(BF16)
    return hi, mid, lo


def _tri_cumsum(tri, v):
    hi, mid, lo = _split3(v)
    return _dot(tri, hi) + _dot(tri, mid) + _dot(tri, lo)


def _tril_incl(n):
    r = lax.broadcasted_iota(jnp.int32, (n, n), 0)
    c = lax.broadcasted_iota(jnp.int32, (n, n), 1)
    return c <= r


def _norm_proj_kernel(x_ref, nw_ref, w_ref, ws_ref, o_ref, os_ref, *, tn):
    xn = _rms(x_ref[...], nw_ref[...]).astype(BF16)
    os_ref[...] = _dot(xn, ws_ref[...])
    for j in range(w_ref.shape[1] // tn):
        cols = slice(j * tn, (j + 1) * tn)
        o_ref[:, cols] = _dot(xn, w_ref[:, cols]).astype(o_ref.dtype)


def _resident(shape):
    return pl.BlockSpec(shape, lambda *_: (0,) * len(shape), pipeline_mode=pl.Buffered(1))


def _norm_proj(x, nw, w_main, w_small, *, tm, tn):
    t, d = x.shape
    n = w_main.shape[1]
    ns = w_small.shape[1]
    return pl.pallas_call(
        functools.partial(_norm_proj_kernel, tn=tn),
        grid=(t // tm,),
        in_specs=[
            pl.BlockSpec((tm, d), lambda i: (i, 0)),
            _resident((1, d)),
            _resident((d, n)),
            _resident((d, ns)),
        ],
        out_specs=[
            pl.BlockSpec((tm, n), lambda i: (i, 0)),
            pl.BlockSpec((tm, ns), lambda i: (i, 0)),
        ],
        out_shape=[jax.ShapeDtypeStruct((t, n), BF16), jax.ShapeDtypeStruct((t, ns), F32)],
        compiler_params=pltpu.CompilerParams(
            dimension_semantics=("parallel",), vmem_limit_bytes=VMEM_LIMIT),
        name="norm_proj",
    )(x, nw, w_main, w_small)


def _ssd_kernel(z_ref, x_ref, bc_ref, xp_ref, bcp_ref, dt_ref, cw_ref, cb_ref, dtb_ref, alog_ref, dsk_ref,
                nw_ref, ex_ref, sh_ref, o_ref, state_ref):
    c = pl.program_id(1)
    l = CHUNK
    p = SSD_HEAD_DIM
    hpg = SSD_HEADS // SSD_GROUPS

    @pl.when(c == 0)
    def _():
        state_ref[...] = jnp.zeros(state_ref.shape, F32)

    cur = jnp.concatenate([x_ref[...], bc_ref[...]], axis=1)
    prev = jnp.concatenate([xp_ref[...], bcp_ref[...]], axis=1)
    prev = jnp.where(c > 0, prev, jnp.zeros_like(prev))
    xin = jnp.concatenate([prev, cur], axis=0)
    conv = cb_ref[...] + cw_ref[SSD_CONV - 1:SSD_CONV, :] * cur.astype(F32)
    for k in range(SSD_CONV - 1):
        conv = conv + cw_ref[k:k + 1, :] * _dot(sh_ref[k], xin)
    xbc = _silu(conv)
    xs = xbc[:, 0:SSD_INNER]

    dt = _softplus(dt_ref[...] + dtb_ref[...])
    da = dt * (-jnp.exp(alog_ref[...]))
    tri = _tril_incl(l)
    acs = _tri_cumsum(tri.astype(BF16), da)
    acs_t = acs.T
    last = acs[l - 1:l, :]
    eacs = jnp.exp(acs)
    dtd = dt * jnp.exp(last - acs)

    def expand(v):
        hi = v.astype(BF16)
        lo = (v - hi.astype(F32)).astype(BF16)
        return _dot(jnp.concatenate([hi, lo], axis=1), ex_ref[...])

    dt_x = expand(dt)
    dtd_x = expand(dtd)
    eacs_x = expand(eacs)
    cdec_x = eacs_x[l - 1:l, :]
    xd = xs * dt_x
    xdd = (xs * dtd_x).astype(BF16)

    lane = lax.broadcasted_iota(jnp.int32, (l, LANES), 1)
    first = lane < p
    gw = hpg * p
    y_parts = []
    for g in range(SSD_GROUPS):
        bm = xbc[:, SSD_INNER + g * SSD_STATE:SSD_INNER + (g + 1) * SSD_STATE]
        cm = xbc[:, SSD_INNER + (SSD_GROUPS + g) * SSD_STATE:SSD_INNER + (SSD_GROUPS + g + 1) * SSD_STATE]
        cm_b = cm.astype(BF16)
        cb = _dot_nt(cm_b, bm.astype(BF16))
        cols = slice(g * gw, (g + 1) * gw)
        st = state_ref[g]
        y_off = _dot(cm_b, st.astype(BF16)) * eacs_x[:, cols]
        state_ref[g] = cdec_x[:, cols] * st + _dot(bm.T.astype(BF16), xdd[:, cols])
        for j in range(hpg // 2):
            h0 = g * hpg + 2 * j
            ms = []
            for h in (h0, h0 + 1):
                seg = acs[:, h:h + 1] - acs_t[h:h + 1, :]
                ms.append((cb * jnp.exp(jnp.where(tri, seg, -jnp.inf))).astype(BF16))
            xp = xd[:, h0 * p:(h0 + 2) * p]
            rhs = jnp.concatenate([jnp.where(first, xp, 0.0), jnp.where(first, 0.0, xp)], axis=0)
            y_parts.append(_dot(jnp.concatenate(ms, axis=1), rhs.astype(BF16))
                           + y_off[:, 2 * j * p:(2 * j + 2) * p])

    y = jnp.concatenate(y_parts, axis=1) + xs * dsk_ref[...]
    y = y * _silu(z_ref[...].astype(F32))
    o_ref[...] = _rms(y, nw_ref[...]).astype(o_ref.dtype)


def _ssd(p0, dt_raw, conv_w, conv_b, dt_bias, a_log, d_skip, norm_w, *, bsz, seq):
    nc = seq // CHUNK
    l = CHUNK
    row = lambda b, c: b * nc + c
    prev_row = lambda b, c: jnp.maximum(b * nc + c - 1, 0)
    full = lambda shape: pl.BlockSpec(shape, lambda b, c: (0, 0))
    head_of_col =jnp.arange(SSD_INNER)[None, :] // SSD_HEAD_DIM
    expand = (jnp.arange(2 * LANES)[:, None] % LANES == head_of_col).astype(BF16)
    delay = (SSD_CONV - 1) - jnp.arange(SSD_CONV - 1)[:, None, None]
    shifts = (jnp.arange(2 * l)[None, None, :] == l + jnp.arange(l)[None, :, None] - delay).astype(BF16)
    return pl.pallas_call(
        _ssd_kernel,
        grid=(bsz, nc),
        in_specs=[
            pl.BlockSpec((l, SSD_INNER), lambda b, c: (row(b, c), P0_Z // SSD_INNER)),
            pl.BlockSpec((l, SSD_INNER), lambda b, c: (row(b, c), P0_X // SSD_INNER)),
            pl.BlockSpec((l, SSD_BC), lambda b, c: (row(b, c), P0_BC // SSD_BC)),
            pl.BlockSpec((l, SSD_INNER), lambda b, c: (prev_row(b, c), P0_X // SSD_INNER)),
            pl.BlockSpec((l, SSD_BC), lambda b, c: (prev_row(b, c), P0_BC // SSD_BC)),
            pl.BlockSpec((l, LANES), lambda b, c: (row(b, c), 0)),
            full((SSD_CONV, SSD_INNER + SSD_BC)),
            full((1, SSD_INNER + SSD_BC)),
            full((1, LANES)),
            full((1, LANES)),
            full((1, SSD_INNER)),
            full((1, SSD_INNER)),
            full((2 * LANES, SSD_INNER)),
            pl.BlockSpec((SSD_CONV - 1, l, 2 * l), lambda b, c: (0, 0, 0)),
        ],
        out_specs=pl.BlockSpec((l, SSD_INNER), lambda b, c: (row(b, c), 0)),
        out_shape=jax.ShapeDtypeStruct((bsz * seq, SSD_INNER), BF16),
        scratch_shapes=[
            pltpu.VMEM((SSD_GROUPS, SSD_STATE, SSD_INNER // SSD_GROUPS), F32),
        ],
        compiler_params=pltpu.CompilerParams(
            dimension_semantics=("parallel", "arbitrary"), vmem_limit_bytes=VMEM_LIMIT),
        name="ssd_scan",
    )(p0, p0, p0, p0, p0, dt_raw, conv_w, conv_b, dt_bias, a_log, d_skip, norm_w, expand, shifts)


def _head_rms(x, w):
    dh = SB_HEAD_DIM
    lane = lax.broadcasted_iota(jnp.int32, (x.shape[0], LANES), 1)
    lo = lane < dh
    outs = []
    for s in range(x.shape[1] // LANES):
        xs = x[:, s * LANES:(s + 1) * LANES]
        x2 = xs * xs
        s0 = jnp.sum(jnp.where(lo, x2, 0.0), axis=-1, keepdims=True)
        s1 = jnp.sum(jnp.where(lo, 0.0, x2), axis=-1, keepdims=True)
        ms = jnp.where(lo, s0, s1) * (1.0 / dh)
        outs.append(xs * lax.rsqrt(ms + RMS_EPS))
    return jnp.concatenate(outs, axis=1) * w


def _sb_kernel(q_ref, k_ref, v_ref, qw_ref, kw_ref, w2_ref, o_ref, kt_ref, vz_ref, carry_ref, acc_ref,
               lw_ref, tot_ref, *, nkb):
    qi = pl.program_id(2)
    tq = SB_TILE
    sub = CHUNK
    dh = SB_HEAD_DIM
    nh = SB_HEADS_PER_STEP
    width = nh * dh
    gh = SB_PV_HEADS
    gwid = gh * dh

    @pl.when(qi == 0)
    def _():
        head_of_lane = lax.broadcasted_iota(jnp.int32, (tq, gwid), 1) // dh
        for kb in range(nkb):
            rows = slice(kb * tq, (kb + 1) * tq)
            kn = _head_rms(k_ref[rows, :].astype(F32), kw_ref[...])
            kt_ref[kb] = kn.T.astype(BF16)
            for gi in range(nh // gh):
                v = v_ref[rows, gi * gwid:(gi + 1) * gwid].astype(F32)
                for hh in range(gh):
                    vz_ref[kb, gi, hh * tq:(hh + 1) * tq, :] = (
                        jnp.where(head_of_lane == hh, v, 0.0).astype(BF16))

    qn = (_head_rms(q_ref[...].astype(F32), qw_ref[...]) * (dh ** -0.5 * LOG2E)).astype(BF16)
    w2 = w2_ref[...]
    row = lax.broadcasted_iota(jnp.int32, (tq, tq), 0)
    col = lax.broadcasted_iota(jnp.int32, (tq, tq), 1)
    causal = col < row

    def logits(kb):
        kt = kt_ref[kb]
        return [_dot(qn[:, hh * dh:(hh + 1) * dh], kt[hh * dh:(hh + 1) * dh, :]) for hh in range(nh)]

    def stage_logits(zs, diag):
        rs = []
        for hh in range(nh):
            z = zs[hh]
            sp = jnp.maximum(z, 0.0) + jnp.log2(1.0 + jnp.exp2(_neg_abs(z)))
            spm = jnp.where(causal, sp, 0.0) if diag else sp
            rem = _dot(spm.astype(BF16), w2)
            rs.append((z - sp, rem, jnp.sum(spm, axis=-1, keepdims=True)))
        for hh in range(nh):
            base, rem, tot = rs[hh]
            lw = base + rem
            lw_ref[hh] = jnp.where(causal, lw, -jnp.inf) if diag else lw
            tot_ref[hh] = jnp.broadcast_to(-tot, (tq, sub))

    def weights():
        ws = []
        for hh in range(nh):
            c = carry_ref[hh]
            lw = lw_ref[hh]
            w = jnp.exp2(jnp.concatenate([lw[:, :sub] + c, lw[:, sub:] + c], axis=1))
            carry_ref[hh] = c + tot_ref[hh]
            ws.append(w.astype(BF16))
        return ws

    def pv_dot(ws, kb):
        return jnp.concatenate(
            [_dot(jnp.concatenate(ws[gi * gh:(gi + 1) * gh], axis=1), vz_ref[kb, gi])
             for gi in range(nh // gh)], axis=1)

    carry_ref[...] = jnp.zeros(carry_ref.shape, F32)
    acc_ref[...] = jnp.zeros(acc_ref.shape, F32)
    stage_logits(logits(qi), True)

    def step(j):
        zs = logits(qi - 1 - j)
        pv = pv_dot(weights(), qi - j)
        stage_logits(zs, False)
        acc_ref[...] += pv

    odd = qi & 1

    @pl.when(odd == 1)
    def _():
        step(0)

    def body(i, c):
        j = odd + 2 * i
        step(j)
        step(j + 1)
        return c

    lax.fori_loop(0, lax.shift_right_logical(qi, 1), body, 0)
    acc_ref[...] += pv_dot(weights(), 0)
    o_ref[...] = acc_ref[...].astype(o_ref.dtype)


def _stick_breaking(p0, q_norm_w, k_norm_w, *, bsz, seq):
    tq = SB_TILE
    sub = CHUNK
    nh = SB_HEADS_PER_STEP
    width = nh * SB_HEAD_DIM
    nq = seq // tq
    w2 = jnp.where(jnp.arange(tq)[:, None] > jnp.arange(tq)[None, :], -1.0, 0.0).astype(BF16)
    return pl.pallas_call(
        functools.partial(_sb_kernel, nkb=nq),
        grid=(bsz, SB_HEADS // nh, nq),
        in_specs=[
            pl.BlockSpec((tq, width), lambda b, h, i: (b * nq + i, P0_Q // width + h)),
            pl.BlockSpec((seq, width), lambda b, h, i: (b, P0_K // width + h)),
            pl.BlockSpec((seq, width), lambda b, h, i: (b, P0_V // width + h)),
            pl.BlockSpec((1, width), lambda b, h, i: (0, 0)),
            pl.BlockSpec((1, width), lambda b, h, i: (0, 0)),
            pl.BlockSpec((tq, tq), lambda b, h, i: (0, 0)),
        ],
        out_specs=pl.BlockSpec((tq, width), lambda b, h, i: (b * nq + i, h)),
        out_shape=jax.ShapeDtypeStruct((bsz * seq, SB_INNER), BF16),
        scratch_shapes=[
            pltpu.VMEM((nq, width, tq), BF16),
            pltpu.VMEM((nq, nh // SB_PV_HEADS, SB_PV_HEADS * tq, SB_PV_HEADS * SB_HEAD_DIM), BF16),
            pltpu.VMEM((nh, tq, sub), F32),
            pltpu.VMEM((tq, width), F32),
            pltpu.VMEM((nh, tq, tq), F32),
            pltpu.VMEM((nh, tq, sub), F32),
        ],
        compiler_params=pltpu.CompilerParams(
            dimension_semantics=("parallel", "parallel", "arbitrary"), vmem_limit_bytes=VMEM_LIMIT),
        name="stick_breaking",
    )(p0, p0, p0, q_norm_w, k_norm_w, w2)


def _out_proj_kernel(*refs, n_in):
    a_refs = refs[:n_in]
    w_refs = refs[n_in:2 * n_in]
    x_ref = refs[2 * n_in]
    o_ref = refs[2 * n_in + 1]
    acc = x_ref[...]
    for a_ref, w_ref in zip(a_refs, w_refs):
        acc = acc + _dot(a_ref[...], w_ref[...])
    o_ref[...] = acc


def _out_proj(acts, weights, x, *, tm):
    t, d = x.shape
    n_in = len(acts)
    in_specs = [pl.BlockSpec((tm, a.shape[1]), lambda i: (i, 0)) for a in acts]
    in_specs += [_resident(w.shape) for w in weights]
    in_specs += [pl.BlockSpec((tm, d), lambda i: (i, 0))]
    return pl.pallas_call(
        functools.partial(_out_proj_kernel, n_in=n_in),
        grid=(t // tm,),
        in_specs=in_specs,
        out_specs=pl.BlockSpec((tm, d), lambda i: (i, 0)),
        out_shape=jax.ShapeDtypeStruct((t, d), F32),
        compiler_params=pltpu.CompilerParams(
            dimension_semantics=("parallel",), vmem_limit_bytes=VMEM_LIMIT),
        name="out_proj",
    )(*acts, *weights, x)


def _ffn_kernel(h_ref, halo_ref, nw_ref, upw_ref, cw_ref, cb_ref, dw_ref, o_ref, *, tm, seq):
    i = pl.program_id(0)
    hr = BF16_ROWS
    h = h_ref[...]
    keep = jnp.where(lax.rem(i * tm, seq) == 0, 0.0, 1.0)
    hn_ext = jnp.concatenate(
        [_rms(halo_ref[...], nw_ref[...]) * keep, _rms(h, nw_ref[...])], axis=0).astype(BF16)
    us = [_dot(hn_ext, upw_ref[:, base:base + D_FF]) for base in (0, D_FF)]
    ys = []
    for u, base in zip(us, (0, D_FF)):
        sl = slice(base, base + D_FF)
        y = cb_ref[:, sl]
        for k in range(FFN_CONV):
            off = hr - (FFN_CONV - 1) + k
            y = y + cw_ref[k:k + 1, sl] * u[off:off + tm, :]
        ys.append(y)
    act = (_silu(ys[0]) * ys[1]).astype(BF16)
    o_ref[...] = h + _dot(act, dw_ref[...])


def _conv_ffn(h, nw, up_w, conv_w, conv_b, down_w, *, seq, tm):
    t, d = h.shape
    hr = BF16_ROWS
    const = _resident
    return pl.pallas_call(
        functools.partial(_ffn_kernel, tm=tm, seq=seq),
        grid=(t // tm,),
        in_specs=[
            pl.BlockSpec((tm, d), lambda i: (i, 0)),
            pl.BlockSpec((hr, d), lambda i: (jnp.maximum(i * (tm // hr) - 1, 0), 0)),
            const((1, d)),
            const(up_w.shape),
            const(conv_w.shape),
            const(conv_b.shape),
            const(down_w.shape),
        ],
        out_specs=pl.BlockSpec((tm, d), lambda i: (i, 0)),
        out_shape=jax.ShapeDtypeStruct((t, d), F32),
        compiler_params=pltpu.CompilerParams(
            dimension_semantics=("parallel",), vmem_limit_bytes=VMEM_LIMIT),
        name="conv_ffn",
    )(h, h, nw, up_w, conv_w, conv_b, down_w)


def _mlstm_kernel(q_ref, k_ref, v_ref, og_ref, ig_ref, fg_ref, ib_ref, fb_ref, nw_ref,
                  o_ref, cn_ref, m_ref):
    c = pl.program_id(1)
    l = CHUNK
    kd = ML_QK_DIM
    vd = ML_V_DIM

    @pl.when(c == 0)
    def _():
        cn_ref[...] = jnp.zeros(cn_ref.shape, F32)
        m_ref[...] = jnp.zeros(m_ref.shape, F32)

    cap = lambda x: ML_GATE_CAP * jnp.tanh(x * (1.0 / ML_GATE_CAP))
    logi = cap(ig_ref[...] + ib_ref[...])
    fc = cap(fg_ref[...] + fb_ref[...])
    logf = jnp.minimum(fc, 0.0) - jnp.log1p(jnp.exp(-jnp.abs(fc)))
    tri = _tril_incl(l)
    bcs = _tri_cumsum(tri.astype(BF16), logf)
    m_prev = m_ref[...]

    cm = logi - bcs
    pm = cm
    k = 1
    while k < l:
        pm = jnp.maximum(pm, jnp.concatenate([jnp.full((k, LANES), -jnp.inf, F32), pm[:l - k]], axis=0))
        k *= 2
    mx = jnp.maximum(m_prev, pm)
    u = -mx
    iw = jnp.exp(m_prev - mx)
    emt = jnp.exp(-(bcs + mx))
    glast = bcs[l - 1:l, :]
    w_end = glast + cm
    m_loc = jnp.max(w_end, axis=0, keepdims=True)
    e_end = jnp.exp(w_end - m_loc)
    m_new = jnp.maximum(glast + m_prev, m_loc)
    a_old = jnp.exp(glast + m_prev - m_new)
    a_loc = jnp.exp(m_loc - m_new)
    m_ref[...] = m_new
    cm_t = cm.T

    q = q_ref[...]
    k_all = k_ref[...]
    k_t = k_all.astype(F32).T.astype(BF16)
    ones = jnp.ones((l, vd), BF16)
    heads = range(ML_HEADS)
    col = lambda a, h: jnp.broadcast_to(a[:, h:h + 1], (l, vd))
    qs = [(q[:, h * kd:(h + 1) * kd].astype(F32) * (kd ** -0.5)).astype(BF16) for h in heads]
    vs = [v_ref[:, h * vd:(h + 1) * vd] for h in heads]
    cns = [cn_ref[h] for h in heads]
    qk = [_dot_nt(qs[h], k_all[:, h * kd:(h + 1) * kd]) for h in heads]
    qc = [_dot(qs[h], cns[h].astype(BF16)) for h in heads]
    sw = [jnp.where(tri, jnp.exp(col(u, h) + cm_t[h:h + 1, :]), 0.0) * qk[h] for h in heads]
    nd = [_dot(sw[h].astype(BF16), jnp.concatenate([vs[h], ones], axis=1)) for h in heads]
    iwc = [col(iw, h) for h in heads]
    num = [nd[h][:, 0:vd] + qc[h][:, 0:vd] * iwc[h] for h in heads]
    den = [nd[h][:, vd:] + qc[h][:, vd:] * iwc[h] for h in heads]
    hm = [num[h] / jnp.maximum(jnp.abs(den[h]), col(emt, h)) for h in heads]
    hm = [_rms(hm[h], nw_ref[:, h * vd:(h + 1) * vd]) for h in heads]
    for h in heads:
        og = og_ref[:, h * vd:(h + 1) * vd].astype(F32)
        o_ref[:, h * vd:(h + 1) * vd] = (hm[h] * _sigmoid(og)).astype(o_ref.dtype)

    ec = [col(e_end, h) for h in heads]
    xe = [jnp.concatenate([vs[h].astype(F32) * ec[h], ec[h]], axis=1).astype(BF16) for h in heads]
    loc = [_dot(k_t[h * kd:(h + 1) * kd, :], xe[h]) for h in heads]
    for h in heads:
        cn_ref[h] = a_old[:, h:h + 1] * cns[h] + a_loc[:, h:h + 1] * loc[h]


def _mlstm(p1, gates, i_b, f_b, norm_w, *, bsz, seq):
    nc = seq // CHUNK
    l = CHUNK
    row = lambda b, c: b * nc + c
    full = lambda shape: pl.BlockSpec(shape, lambda b, c: (0, 0))
    return pl.pallas_call(
        _mlstm_kernel,
        grid=(bsz, nc),
        in_specs=[
            pl.BlockSpec((l, ML_QK), lambda b, c: (row(b, c), 0)),
            pl.BlockSpec((l, ML_QK), lambda b, c: (row(b, c), 1)),
            pl.BlockSpec((l, ML_V), lambda b, c: (row(b, c), 1)),
            pl.BlockSpec((l, ML_V), lambda b, c: (row(b, c), 2)),
            pl.BlockSpec((l, LANES), lambda b, c: (row(b, c), 0)),
            pl.BlockSpec((l, LANES), lambda b, c: (row(b, c), 1)),
            full((1, LANES)),
            full((1, LANES)),
            full((1, ML_V)),
        ],
        out_specs=pl.BlockSpec((l, ML_V), lambda b, c: (row(b, c), 0)),
        out_shape=jax.ShapeDtypeStruct((bsz * seq, ML_V), BF16),
        scratch_shapes=[
            pltpu.VMEM((ML_HEADS, ML_QK_DIM, 2 * ML_V_DIM), F32),
            pltpu.VMEM((1, LANES), F32),
        ],
        compiler_params=pltpu.CompilerParams(
            dimension_semantics=("parallel", "arbitrary"), vmem_limit_bytes=VMEM_LIMIT),
        name="mlstm_scan",
    )(p1, p1, p1, p1, gates, gates, i_b, f_b, norm_w)


def _pad_cols(w, n):
    return jnp.pad(w, ((0, 0), (0, n - w.shape[1])))


def _row(v, n=None):
    v = v.reshape(1, -1).astype(F32)
    return v if n is None else _pad_cols(v, n)


def kernel(x, norm_w, hy_in_w, ssd_conv_w, ssd_conv_b, ssd_dt_bias, ssd_a_log, ssd_d, ssd_norm_w,
           sb_q_norm_w, sb_k_norm_w, hy_out_w, ml_in_w, ml_i_b, ml_f_b, ml_norm_w, ml_out_w,
           ffn_up_w, ffn_conv_w, ffn_conv_b, ffn_down_w):
    bsz, seq, d = x.shape
    t = bsz * seq
    tm = min(512, seq)
    h = x.reshape(t, d)

    w = hy_in_w[0].astype(BF16)
    o_xbc = SSD_INNER
    o_dt = o_xbc + SSD_INNER + SSD_BC
    o_q = o_dt + SSD_HEADS
    w_z = w[:, 0:SSD_INNER]
    w_x = w[:, o_xbc:o_xbc + SSD_INNER]
    w_bc = w[:, o_xbc + SSD_INNER:o_dt]
    w_dt = w[:, o_dt:o_q]
    w_qkv = w[:, o_q:]
    w_main = jnp.concatenate([w_z, w_qkv, w_x, w_bc], axis=1)
    w_small = _pad_cols(w_dt, LANES)
    p0, dt_raw = _norm_proj(h, _row(norm_w[0, 0]), w_main, w_small, tm=tm, tn=512)

    y_ssd = _ssd(p0, dt_raw, ssd_conv_w[0], _row(ssd_conv_b[0]), _row(ssd_dt_bias[0], LANES),
                 _row(ssd_a_log[0], LANES), _row(jnp.repeat(ssd_d[0], SSD_HEAD_DIM)),
                 _row(ssd_norm_w[0]), bsz=bsz, seq=seq)
    y_sb = _stick_breaking(p0, _row(jnp.tile(sb_q_norm_w[0], SB_HEADS_PER_STEP)),
                           _row(jnp.tile(sb_k_norm_w[0], SB_HEADS_PER_STEP)),
                           bsz=bsz, seq=seq)
    w_out = hy_out_w[0].astype(BF16)
    h = _out_proj([y_ssd, y_sb], [w_out[0:SSD_INNER], w_out[SSD_INNER:]], h, tm=tm)
    h = _conv_ffn(h, _row(norm_w[0, 1]), ffn_up_w[0].astype(BF16), ffn_conv_w[0], _row(ffn_conv_b[0]),
                  ffn_down_w[0].astype(BF16), seq=seq, tm=tm)

    w = ml_in_w[0]
    o_g = 2 * ML_QK + 2 * ML_V
    w_main = w[:, 0:o_g].astype(BF16)
    w_small = jnp.concatenate(
        [_pad_cols(w[:, o_g:o_g + ML_HEADS], LANES), _pad_cols(w[:, o_g + ML_HEADS:], LANES)],
        axis=1).astype(BF16)
    p1, gates = _norm_proj(h, _row(norm_w[1, 0]), w_main, w_small, tm=tm, tn=512)
    y_ml = _mlstm(p1, gates, _row(ml_i_b[0], LANES), _row(ml_f_b[0], LANES), _row(ml_norm_w[0]),
                  bsz=bsz, seq=seq)
    h = _out_proj([y_ml], [ml_out_w[0].astype(BF16)], h, tm=tm)
    h = _conv_ffn(h, _row(norm_w[1, 1]), ffn_up_w[1].astype(BF16), ffn_conv_w[1], _row(ffn_conv_b[1]),
                  ffn_down_w[1].astype(BF16), seq=seq, tm=tm)
    return h.reshape(bsz, seq, d)
```
